```python
import jax, jax.numpy as jnp
from jax import lax
import numpy as np

D_MODEL = 1024
BATCH = 32
SEQ = 2048
DEPTH = 2

CTX_LEN = 256
GRID_W = 64
CHUNK = 128
GM_GROUPS = 8
GM_WIDTH = 1024
GM_GROUP_DIM = GM_WIDTH // GM_GROUPS
N_HEADS = 8
Q_LORA = 384
KV_LORA = 256
QK_NOPE = 128
QK_ROPE = 64
V_HEAD = 128
ROPE_THETA = 10000.0
Q_BLOCK = 128
N_EXPERTS = 16
N_EXPERT_GROUPS = 4
EXPERTS_PER_GROUP = N_EXPERTS // N_EXPERT_GROUPS
TOP_K = 2
D_EXPERT = 512
MOE_BLOCK = 512
LN_EPS = 1e-5
ALPHA = (2.0 * DEPTH) ** 0.25
BETA = (8.0 * DEPTH) ** -0.25

OFF_Q = 2 * GM_WIDTH
OFF_KV = OFF_Q + Q_LORA
OFF_KR = OFF_KV + KV_LORA
OFF_G = OFF_KR + QK_ROPE
IN_COLS = OFF_G + 2 * D_MODEL

kernel_name = "hybrid_gmlp_mla_moe_diffusion_block"


def layer_norm(x, g, b):
    x32 = x.astype(jnp.float32)
    mu = x32.mean(-1, keepdims=True)
    var = jnp.square(x32 - mu).mean(-1, keepdims=True)
    return ((x32 - mu) * lax.rsqrt(var + LN_EPS) * g + b).astype(x.dtype)


def rms_norm(x, g):
    x32 = x.astype(jnp.float32)
    return (x32 * lax.rsqrt(jnp.square(x32).mean(-1, keepdims=True) + LN_EPS) * g).astype(x.dtype)


def axial_rope(length, dtype):
    rows = length // GRID_W
    row = jnp.repeat(jnp.arange(rows), GRID_W).astype(jnp.float32)
    col = jnp.tile(jnp.arange(GRID_W), rows).astype(jnp.float32)
    n_freq = QK_ROPE // 4
    inv = ROPE_THETA ** (-jnp.arange(n_freq, dtype=jnp.float32) / n_freq)
    ang = jnp.concatenate([row[:, None] * inv, col[:, None] * inv], axis=-1)
    return jnp.cos(ang).astype(dtype), jnp.sin(ang).astype(dtype)


def apply_rope(x, cos, sin):
    half = x.shape[-1] // 2
    x1, x2 = x[..., :half], x[..., half:]
    return jnp.concatenate([x1 * cos - x2 * sin, x1 * sin + x2 * cos], axis=-1)


def mla_q(q_a, q_norm, w_uq, rope):
    q = (rms_norm(q_a, q_norm) @ w_uq).reshape(*q_a.shape[:2], N_HEADS, QK_NOPE + QK_ROPE)
    q_nope, q_rope = q[..., :QK_NOPE], q[..., QK_NOPE:]
    if rope is not None:
        q_rope = apply_rope(q_rope, rope[0][:, None, :], rope[1][:, None, :])
    return q_nope, q_rope


def mla_kv(p_kv, kv_norm, w_ukv, rope):
    kv_a, k_rope = p_kv[..., :KV_LORA], p_kv[..., KV_LORA:]
    kv = (rms_norm(kv_a, kv_norm) @ w_ukv).reshape(*p_kv.shape[:2], N_HEADS, QK_NOPE + V_HEAD)
    if rope is not None:
        k_rope = apply_rope(k_rope, rope[0], rope[1])
    return kv[..., :QK_NOPE], k_rope, kv[..., QK_NOPE:]


def attend(q_nope, q_rope, k_nope, k_rope, v):
    scale = (QK_NOPE + QK_ROPE) ** -0.5
    s = jnp.einsum('bqhd,bkhd->bhqk', q_nope, k_nope) + jnp.einsum('bqhr,bkr->bhqk', q_rope, k_rope)
    p = jax.nn.softmax(s.astype(jnp.float32) * scale, axis=-1).astype(v.dtype)
    return jnp.einsum('bhqk,bkhd->bqhd', p, v)


def blocked_attend(q_nope, q_rope, k_nope, k_rope, v):
    b, lq = q_nope.shape[:2]
    nblk = lq // Q_BLOCK

    def to_blocks(t):
        return jnp.moveaxis(t.reshape(b, nblk, Q_BLOCK, *t.shape[2:]), 1, 0)

    out = lax.map(lambda qs: attend(qs[0], qs[1], k_nope, k_rope, v), (to_blocks(q_nope), to_blocks(q_rope)))
    return jnp.moveaxis(out, 0, 1).reshape(b, lq, N_HEADS, V_HEAD)


def chunk_gmlp(u, v, ln_g, ln_b, w_s, b_s):
    b, l, _ = u.shape
    vn = layer_norm(v, ln_g, ln_b).reshape(b, l // CHUNK, CHUNK, GM_GROUPS, GM_GROUP_DIM)
    s = jnp.einsum('gij,bnjgc->bnigc', w_s, vn) + b_s.T[:, :, None]
    return u * s.reshape(b, l, GM_WIDTH)


def token_mix(p, k_nope, k_rope, v, rope, q_norm, w_uq, gm_ln_g, gm_ln_b, gm_ws, gm_bs,
              w_proj_a, w_proj_b, b_merge, w_out):
    q_nope, q_rope = mla_q(p[..., OFF_Q:OFF_KV], q_norm, w_uq, rope)
    att = blocked_attend(q_nope, q_rope, k_nope, k_rope, v)
    z = jax.nn.gelu(p[..., :2 * GM_WIDTH])
    gm = chunk_gmlp(z[..., :GM_WIDTH], z[..., GM_WIDTH:], gm_ln_g, gm_ln_b, gm_ws, gm_bs)
    gates = jax.nn.sigmoid(p[..., OFF_G:] + b_merge)
    y = (gates[..., :D_MODEL] * (gm @ w_proj_a)
         + gates[..., D_MODEL:] * (att.reshape(*att.shape[:2], N_HEADS * V_HEAD) @ w_proj_b))
    return y @ w_out


def moe(h, w_router, b_router, w_gate, w_up, w_down):
    shp = h.shape
    hf = h.reshape(-1, D_MODEL)
    t = hf.shape[0]
    aff = jax.nn.sigmoid((hf @ w_router).astype(jnp.float32))
    sel = (aff + b_router.astype(jnp.float32)).reshape(t, N_EXPERT_GROUPS, EXPERTS_PER_GROUP)
    group_score = lax.top_k(sel, TOP_K)[0].sum(-1)
    grp = jnp.argmax(group_score, axis=-1)
    sel_in = jnp.take_along_axis(sel, grp[:, None, None], axis=1)[:, 0]
    _, idx = lax.top_k(sel_in, TOP_K)
    expert = grp[:, None] * EXPERTS_PER_GROUP + idx
    wts = jnp.take_along_axis(aff, expert, axis=1)
    wts = wts / wts.sum(-1, keepdims=True)
    flat_e = expert.reshape(-1)
    n_assign = flat_e.shape[0]
    order = jnp.argsort(flat_e)
    e_sorted = flat_e[order]
    tok = order // TOP_K
    sizes = jnp.bincount(flat_e, length=N_EXPERTS)
    starts = jnp.cumsum(sizes) - sizes
    padded = (sizes + MOE_BLOCK - 1) // MOE_BLOCK * MOE_BLOCK
    pad_ends = jnp.cumsum(padded)
    pad_starts = pad_ends - padded
    dest = pad_starts[e_sorted] + jnp.arange(n_assign) - starts[e_sorted]
    n_blocks = -(-(n_assign + N_EXPERTS * (MOE_BLOCK - 1)) // MOE_BLOCK)
    buf = jnp.zeros((n_blocks * MOE_BLOCK, D_MODEL), h.dtype).at[dest].set(hf[tok])
    blk_expert = jnp.minimum(
        jnp.searchsorted(pad_ends, jnp.arange(n_blocks) * MOE_BLOCK, side='right'), N_EXPERTS - 1)

    def expert_ffn(args):
        xb, e = args
        return (jax.nn.silu(xb @ w_gate[e]) * (xb @ w_up[e])) @ w_down[e]

    out = lax.map(expert_ffn, (buf.reshape(n_blocks, MOE_BLOCK, D_MODEL), blk_expert)).reshape(-1, D_MODEL)
    contrib = out[dest] * wts.reshape(-1)[order][:, None].astype(h.dtype)
    return jax.ops.segment_sum(contrib, tok, num_segments=t).reshape(shp)


def setup_inputs(seed: int = 0) -> dict:
    key = jax.random.key(seed)
    ks = iter(jax.random.split(key, 32))

    def nrm(shape, scale):
        return jax.random.normal(next(ks), shape, jnp.float32) * scale

    D, L = D_MODEL, DEPTH
    return {
        "x": nrm((BATCH, SEQ, D), 1.0),
        "c": nrm((BATCH, D), 1.0),
        "ctx": nrm((BATCH, CTX_LEN, D), 1.0),
        "c_ctx": nrm((D,), 1.0),
        "w_ada": nrm((L, D, 6 * D), 0.5 * D ** -0.5),
        "b_ada": nrm((L, 6 * D), 0.02),
        "w_in": nrm((L, D, IN_COLS), D ** -0.5),
        "gm_ln_g": 1.0 + nrm((L, GM_WIDTH), 0.02),
        "gm_ln_b": nrm((L, GM_WIDTH), 0.02),
        "gm_ws": nrm((L, GM_GROUPS, CHUNK, CHUNK), CHUNK ** -0.5),
        "gm_bs": 1.0 + nrm((L, GM_GROUPS, CHUNK), 0.02),
        "q_norm": 1.0 + nrm((L, Q_LORA), 0.02),
        "w_uq": nrm((L, Q_LORA, N_HEADS * (QK_NOPE + QK_ROPE)), Q_LORA ** -0.5),
        "kv_norm": 1.0 + nrm((L, KV_LORA), 0.02),
        "w_ukv": nrm((L, KV_LORA, N_HEADS * (QK_NOPE + V_HEAD)), KV_LORA ** -0.5),
        "w_proj_a": nrm((L, GM_WIDTH, D), GM_WIDTH ** -0.5),
        "w_proj_b": nrm((L, N_HEADS * V_HEAD, D), (N_HEADS * V_HEAD) ** -0.5),
        "b_merge": nrm((L, 2 * D), 0.02),
        "w_out": nrm((L, D, D), BETA * D ** -0.5),
        "ln1_g": 1.0 + nrm((L, D), 0.02),
        "ln1_b": nrm((L, D), 0.02),
        "w_router": nrm((D, N_EXPERTS), D ** -0.5),
        "b_router": nrm((N_EXPERTS,), 0.01),
        "w_e_gate": nrm((L, N_EXPERTS, D, D_EXPERT), D ** -0.5),
        "w_e_up": nrm((L, N_EXPERTS, D, D_EXPERT), D ** -0.5),
        "w_e_down": nrm((L, N_EXPERTS, D_EXPERT, D), BETA * D_EXPERT ** -0.5),
        "ln2_g": 1.0 + nrm((L, D), 0.02),
        "ln2_b": nrm((L, D), 0.02),
    }


def reference(x, c, ctx, c_ctx, w_ada, b_ada, w_in, gm_ln_g, gm_ln_b, gm_ws, gm_bs, q_norm, w_uq,
              kv_norm, w_ukv, w_proj_a, w_proj_b, b_merge, w_out, ln1_g, ln1_b, w_router, b_router,
              w_e_gate, w_e_up, w_e_down, ln2_g, ln2_b):
    ctx_len = ctx.shape[1]
    rope = axial_rope(x.shape[1], x.dtype)
    lat, cx = x, ctx
    for l in range(DEPTH):
        last = l == DEPTH - 1
        mod = jax.nn.silu(c) @ w_ada[l] + b_ada[l]
        mod_c = jax.nn.silu(c_ctx) @ w_ada[l] + b_ada[l]
        sh1, sc1, g1, sh2, sc2, g2 = jnp.split(mod[:, None, :], 6, axis=-1)
        csh1, csc1, cg1, csh2, csc2, cg2 = jnp.split(mod_c, 6)
        mix_w = (q_norm[l], w_uq[l], gm_ln_g[l], gm_ln_b[l], gm_ws[l], gm_bs[l],
                 w_proj_a[l], w_proj_b[l], b_merge[l], w_out[l])

        hc = cx * (1 + csc1) + csh1
        if last:
            pc_kv = hc @ w_in[l][:, OFF_KV:OFF_G]
        else:
            pc = hc @ w_in[l]
            pc_kv = pc[..., OFF_KV:OFF_G]
        ck_nope, ck_rope, cv = mla_kv(pc_kv, kv_norm[l], w_ukv[l], None)

        h = lat * (1 + sc1) + sh1
        p = h @ w_in[l]
        k_nope, k_rope, v = mla_kv(p[..., OFF_KV:OFF_G], kv_norm[l], w_ukv[l], rope)
        mix = token_mix(p,
                        jnp.concatenate([ck_nope, k_nope], axis=1),
                        jnp.concatenate([ck_rope, k_rope], axis=1),
                        jnp.concatenate([cv, v], axis=1),
                        rope, *mix_w)
        lat = layer_norm(ALPHA * lat + g1 * mix, ln1_g[l], ln1_b[l])

        h2 = lat * (1 + sc2) + sh2
        if last:
            f = moe(h2, w_router, b_router, w_e_gate[l], w_e_up[l], w_e_down[l])
        else:
            cmix = token_mix(pc, ck_nope, ck_rope, cv, None, *mix_w)
            cx = layer_norm(ALPHA * cx + cg1 * cmix, ln1_g[l], ln1_b[l])
            hc2 = cx * (1 + csc2) + csh2
            f_all = moe(jnp.concatenate([hc2, h2], axis=1), w_router, b_router,
                        w_e_gate[l], w_e_up[l], w_e_down[l])
            cx = layer_norm(ALPHA * cx + cg2 * f_all[:, :ctx_len], ln2_g[l], ln2_b[l])
            f = f_all[:, ctx_len:]
        lat = layer_norm(ALPHA * lat + g2 * f, ln2_g[l], ln2_b[l])
    return lat
```

```python
import functools
import math

import jax
import jax.numpy as jnp
from jax import lax
from jax.experimental import pallas as pl
from jax.experimental.pallas import tpu as pltpu

F32 = jnp.float32
BF16 = jnp.bfloat16

D_MODEL = 1024
DEPTH = 2
GRID_W = 64
CHUNK = 128
GM_GROUPS = 8
GM_WIDTH = 1024
N_HEADS = 8
Q_LORA = 384
KV_LORA = 256
QK_NOPE = 128
QK_ROPE = 64
V_HEAD = 128
ROPE_THETA = 10000.0
N_EXPERTS = 16
N_EXPERT_GROUPS = 4
EXPERTS_PER_GROUP = N_EXPERTS // N_EXPERT_GROUPS
TOP_K = 2
D_EXPERT = 512
LN_EPS = 1e-5
ALPHA = (2.0 * DEPTH) ** 0.25

OFF_Q = 2 * GM_WIDTH
OFF_KV = OFF_Q + Q_LORA
OFF_KR = OFF_KV + KV_LORA
OFF_G = OFF_KR + QK_ROPE

QK_PAD = 2 * QK_NOPE
LANES = 128
ROW_TILE = 256
MOE_TILE = 512
VMEM_LIMIT = 56 * 1024 * 1024

HIGHEST = lax.Precision.HIGHEST


def _dot(a, b):
    return jnp.dot(a, b, preferred_element_type=F32)


def _layer_norm(z, g, b):
    mu = jnp.mean(z, axis=-1, keepdims=True)
    zc = z - mu
    var = jnp.mean(zc * zc, axis=-1, keepdims=True)
    return zc * lax.rsqrt(var + LN_EPS) * g + b


def _rms_norm(z, g):
    return z * lax.rsqrt(jnp.mean(z * z, axis=-1, keepdims=True) + LN_EPS) * g


def _ada_kernel(c_ref, w_ref, b_ref, o_ref):
    cc = c_ref[...]
    s = cc * jax.nn.sigmoid(cc)
    o_ref[...] = jnp.dot(s, w_ref[...], precision=HIGHEST, preferred_element_type=F32) + b_ref[...]


def _ada(cc, w_ada, b_ada):
    depth, d, n = w_ada.shape
    rows = cc.shape[0]
    tn = 1536
    return pl.pallas_call(
        _ada_kernel,
        out_shape=jax.ShapeDtypeStruct((depth, rows, n), F32),
        grid=(depth, n // tn),
        in_specs=[
            pl.BlockSpec((rows, d), lambda l, j: (0, 0)),
            pl.BlockSpec((None, d, tn), lambda l, j: (l, 0, j)),
            pl.BlockSpec((None, 1, tn), lambda l, j: (l, 0, j)),
        ],
        out_specs=pl.BlockSpec((None, rows, tn), lambda l, j: (l, 0, j)),
        compiler_params=pltpu.CompilerParams(
            dimension_semantics=("parallel", "parallel"), vmem_limit_bytes=VMEM_LIMIT),
        name="ada",
    )(cc, w_ada, b_ada.reshape(depth, 1, n))


def _inproj_kernel(x_ref, sc_ref, sh_ref, wu_ref, wv_ref, wq_ref, wkv_ref, wkr_ref, wg_ref,
                   lng_ref, lnb_ref, ws_ref, bs_ref, qn_ref, wuq_ref, kvn_ref, wukv_ref, bm_ref,
                   tab_ref, gm_ref, q_ref, k_ref, v_ref, gate_ref):
    tm = x_ref.shape[0]
    h = (x_ref[...] * (1.0 + sc_ref[...]) + sh_ref[...]).astype(BF16)

    u = jax.nn.gelu(_dot(h, wu_ref[...]))
    v = jax.nn.gelu(_dot(h, wv_ref[...]))
    vn = _layer_norm(v, lng_ref[...], lnb_ref[...]).astype(BF16)
    gd = GM_WIDTH // GM_GROUPS
    for c in range(tm // CHUNK):
        rs = slice(c * CHUNK, (c + 1) * CHUNK)
        for g in range(GM_GROUPS):
            cs = slice(g * gd, (g + 1) * gd)
            s = _dot(ws_ref[g], vn[rs, cs]) + bs_ref[g]
            gm_ref[rs, cs] = (u[rs, cs] * s).astype(BF16)

    tab = tab_ref[...]
    half = LANES // 2

    def rope(r):
        m = r * tab
        return m + pltpu.roll(m, half, axis=1)

    scale = (QK_NOPE + QK_ROPE) ** -0.5
    qa = _dot(h, wq_ref[...])
    qn = _rms_norm(qa, qn_ref[...] * scale).astype(BF16)
    pq = _dot(qn, wuq_ref[...])
    for hh in range(N_HEADS):
        o = hh * QK_PAD
        q_ref[hh, :, 0:QK_NOPE] = pq[:, o:o + QK_NOPE].astype(BF16)
        q_ref[hh, :, QK_NOPE:QK_PAD] = rope(pq[:, o + QK_NOPE:o + QK_PAD]).astype(BF16)

    kva = _dot(h, wkv_ref[...])
    kvn = _rms_norm(kva, kvn_ref[...]).astype(BF16)
    pkv = _dot(kvn, wukv_ref[...])
    lane = lax.broadcasted_iota(jnp.int32, (tm, LANES), 1)
    kr = jnp.where(lane < half, rope(_dot(h, wkr_ref[...])), 0.0).astype(BF16)
    for hh in range(N_HEADS):
        o = hh * (QK_NOPE + V_HEAD)
        k_ref[hh, :, 0:QK_NOPE] = pkv[:, o:o + QK_NOPE].astype(BF16)
        k_ref[hh, :, QK_NOPE:QK_PAD] = kr
        v_ref[hh] = pkv[:, o + QK_NOPE:o + QK_NOPE + V_HEAD].astype(BF16)

    gate_ref[...] = jax.nn.sigmoid(_dot(h, wg_ref[...]) + bm_ref[...]).astype(BF16)


def _const_spec(shape):
    nd = len(shape)
    return pl.BlockSpec(shape, lambda b, i: (0,) * nd)


def _inproj(xc, mod4, w, tab, n_batch, ctx_len):
    nb, lt, d = xc.shape
    tm = ROW_TILE
    nct = ctx_len // tm

    def row(b, i):
        return jnp.where(i < nct, n_batch, b)

    weights = [w["wu"], w["wv"], w["wq"], w["wkv"], w["wkr"], w["wg"], w["lng"], w["lnb"], w["ws"],
               w["bs"], w["qn"], w["wuq"], w["kvn"], w["wukv"], w["bm"]]
    in_specs = [
        pl.BlockSpec((None, tm, d), lambda b, i: (b, i, 0)),
        pl.BlockSpec((None, None, 1, d), lambda b, i: (row(b, i), 1, 0, 0)),
        pl.BlockSpec((None, None, 1, d), lambda b, i: (row(b, i), 0, 0, 0)),
    ] + [_const_spec(a.shape) for a in weights] + [pl.BlockSpec((tm, LANES), lambda b, i: (i, 0))]
    out_shape = (
        jax.ShapeDtypeStruct((nb, lt, GM_WIDTH), BF16),
        jax.ShapeDtypeStruct((nb, N_HEADS, lt, QK_PAD), BF16),
        jax.ShapeDtypeStruct((nb, N_HEADS, lt, QK_PAD), BF16),
        jax.ShapeDtypeStruct((nb, N_HEADS, lt, V_HEAD), BF16),
        jax.ShapeDtypeStruct((nb, lt, 2 * d), BF16),
    )
    out_specs = (
        pl.BlockSpec((None, tm, GM_WIDTH), lambda b, i: (b, i, 0)),
        pl.BlockSpec((None, N_HEADS, tm, QK_PAD), lambda b, i: (b, 0, i, 0)),
        pl.BlockSpec((None, N_HEADS, tm, QK_PAD), lambda b, i: (b, 0, i, 0)),
        pl.BlockSpec((None, N_HEADS, tm, V_HEAD), lambda b, i: (b, 0, i, 0)),
        pl.BlockSpec((None, tm, 2 * d), lambda b, i: (b, i, 0)),
    )
    return pl.pallas_call(
        _inproj_kernel,
        out_shape=out_shape,
        grid=(nb, lt // tm),
        in_specs=in_specs,
        out_specs=out_specs,
        compiler_params=pltpu.CompilerParams(
            dimension_semantics=("parallel", "parallel"), vmem_limit_bytes=VMEM_LIMIT),
        name="inproj",
    )(xc, mod4, mod4, *weights, tab)


def _attn_kernel(q_ref, k_ref, v_ref, o_ref, *, n_ctx_tiles, ctx_len, tile_off):
    i = pl.program_id(2) + tile_off

    def run(klen):
        q = q_ref[...]
        k = k_ref[0:klen, :]
        v = v_ref[0:klen, :]
        s = lax.dot_general(q, k, (((1,), (1,)), ((), ())), preferred_element_type=F32)
        m = jnp.max(s, axis=-1, keepdims=True)
        p = jnp.exp(s - m)
        l = jnp.sum(p, axis=-1, keepdims=True)
        o = _dot(p.astype(BF16), v) / l
        o_ref[...] = o.astype(BF16)

    if n_ctx_tiles > 0 and tile_off == 0:
        @pl.when(i < n_ctx_tiles)
        def _():
            run(ctx_len)

        @pl.when(i >= n_ctx_tiles)
        def _():
            run(k_ref.shape[0])
    else:
        run(k_ref.shape[0])


def _attention(q, k, v, ctx_len, latent_only):
    nb, nh, lt, _ = q.shape
    tq = ROW_TILE
    nct = ctx_len // tq
    off = nct if latent_only else 0
    nq = lt // tq - off
    kern = functools.partial(_attn_kernel, n_ctx_tiles=nct, ctx_len=ctx_len, tile_off=off)
    return pl.pallas_call(
        kern,
        out_shape=jax.ShapeDtypeStruct((nb, nq * tq, nh * V_HEAD), BF16),
        grid=(nb, nh, nq),
        in_specs=[
            pl.BlockSpec((None, None, tq, QK_PAD), lambda b, h, i: (b, h, i + off, 0)),
            pl.BlockSpec((None, None, lt, QK_PAD), lambda b, h, i: (b, h, 0, 0)),
            pl.BlockSpec((None, None, lt, V_HEAD), lambda b, h, i: (b, h, 0, 0)),
        ],
        out_specs=pl.BlockSpec((None, tq, V_HEAD), lambda b, h, i: (b, i, h)),
        compiler_params=pltpu.CompilerParams(
            dimension_semantics=("parallel", "parallel", "parallel"), vmem_limit_bytes=VMEM_LIMIT),
        name="attn",
    )(q, k, v)


def _top2_of4(a, b, c, d):
    m1, n1 = jnp.maximum(a, b), jnp.minimum(a, b)
    m2, n2 = jnp.maximum(c, d), jnp.minimum(c, d)
    return jnp.maximum(m1, m2) + jnp.maximum(jnp.minimum(m1, m2), jnp.maximum(n1, n2))


def _merge_kernel(x_ref, gm_ref, att_ref, gate_ref, wa_ref, wb_ref, wo_ref, g1_ref, lg_ref, lb_ref,
                  sc2_ref, sh2_ref, wr_ref, br_ref, lat_ref, h2_ref, eid_ref, ewt_ref):
    d = x_ref.shape[1]
    ya = _dot(gm_ref[...], wa_ref[...])
    yb = _dot(att_ref[...], wb_ref[...])
    y = gate_ref[:, 0:d].astype(F32) * ya + gate_ref[:, d:2 * d].astype(F32) * yb
    mix = _dot(y.astype(BF16), wo_ref[...])
    lat = _layer_norm(ALPHA * x_ref[...] + g1_ref[...] * mix, lg_ref[...], lb_ref[...])
    lat_ref[...] = lat
    h2 = lat * (1.0 + sc2_ref[...]) + sh2_ref[...]
    h2_ref[...] = h2

    logits = jnp.dot(h2, wr_ref[...], precision=HIGHEST, preferred_element_type=F32)
    aff_t = jax.nn.sigmoid(logits.T[0:N_EXPERTS, :])
    sel_t = aff_t + br_ref[...]
    aff = [aff_t[e:e + 1, :] for e in range(N_EXPERTS)]
    sel = [sel_t[e:e + 1, :] for e in range(N_EXPERTS)]
    epg = EXPERTS_PER_GROUP
    best = _top2_of4(*sel[0:epg])
    grp = jnp.zeros_like(best, dtype=jnp.int32)
    for g in range(1, N_EXPERT_GROUPS):
        sc = _top2_of4(*sel[g * epg:(g + 1) * epg])
        better = sc > best
        best = jnp.where(better, sc, best)
        grp = jnp.where(better, g, grp)

    def pick(vals, j):
        out = vals[j]
        for g in range(1, N_EXPERT_GROUPS):
            out = jnp.where(grp == g, vals[g * epg + j], out)
        return out

    s_in = [pick(sel, j) for j in range(epg)]
    a_in = [pick(aff, j) for j in range(epg)]
    i1 = jnp.zeros_like(grp)
    v1, w1 = s_in[0], a_in[0]
    for j in range(1, epg):
        better = s_in[j] > v1
        v1 = jnp.where(better, s_in[j], v1)
        w1 = jnp.where(better, a_in[j], w1)
        i1 = jnp.where(better, j, i1)
    i2 = jnp.full_like(grp, -1)
    v2 = jnp.full_like(v1, -jnp.inf)
    w2 = jnp.zeros_like(w1)
    for j in range(epg):
        better = (i1 != j) & (s_in[j] > v2)
        v2 = jnp.where(better, s_in[j], v2)
        w2 = jnp.where(better, a_in[j], w2)
        i2 = jnp.where(better, j, i2)
    tot = w1 + w2
    eid_ref[0:1, :] = grp * epg + i1
    eid_ref[1:2, :] = grp * epg + i2
    ewt_ref[0:1, :] = w1 / tot
    ewt_ref[1:2, :] = w2 / tot


def _merge(xc, gm, att, gate, mod4, w, wr_pad, br_col, n_batch, ctx_len, latent_only):
    nb, lt, d = xc.shape
    tm = ROW_TILE
    nct = ctx_len // tm
    off = nct if latent_only else 0
    nt = lt // tm - off

    def row(b, i):
        return jnp.where(i + off < nct, n_batch, b)

    def mod_spec(k):
        return pl.BlockSpec((None, None, 1, d), lambda b, i: (row(b, i), k, 0, 0))

    weights = [w["wa"], w["wb"], w["wo"]]
    in_specs = [
        pl.BlockSpec((None, tm, d), lambda b, i: (b, i + off, 0)),
        pl.BlockSpec((None, tm, GM_WIDTH), lambda b, i: (b, i + off, 0)),
        pl.BlockSpec((None, tm, N_HEADS * V_HEAD), lambda b, i: (b, i, 0)),
        pl.BlockSpec((None, tm, 2 * d), lambda b, i: (b, i + off, 0)),
    ] + [_const_spec(a.shape) for a in weights] + [
        mod_spec(2), _const_spec(w["ln1g"].shape), _const_spec(w["ln1b"].shape), mod_spec(4), mod_spec(3),
        _const_spec(wr_pad.shape), _const_spec(br_col.shape)]
    out_shape = (
        jax.ShapeDtypeStruct((nb, nt * tm, d), F32),
        jax.ShapeDtypeStruct((nb, nt * tm, d), F32),
        jax.ShapeDtypeStruct((nb * nt, TOP_K, tm), jnp.int32),
        jax.ShapeDtypeStruct((nb * nt, TOP_K, tm), F32),
    )
    out_specs = (
        pl.BlockSpec((None, tm, d), lambda b, i: (b, i, 0)),
        pl.BlockSpec((None, tm, d), lambda b, i: (b, i, 0)),
        pl.BlockSpec((None, TOP_K, tm), lambda b, i: (b * nt + i, 0, 0)),
        pl.BlockSpec((None, TOP_K, tm), lambda b, i: (b * nt + i, 0, 0)),
    )
    return pl.pallas_call(
        _merge_kernel,
        out_shape=out_shape,
        grid=(nb, nt),
        in_specs=in_specs,
        out_specs=out_specs,
        compiler_params=pltpu.CompilerParams(
            dimension_semantics=("parallel", "parallel"), vmem_limit_bytes=VMEM_LIMIT),
        name="merge",
    )(xc, gm, att, gate, *weights, mod4, w["ln1g"], w["ln1b"], mod4, mod4, wr_pad, br_col)


def _row_gather_start(idx_smem, slot, n_rows, src_hbm, dst_buf, sem):
    def body(j, carry):
        tok = idx_smem[slot, j]
        pltpu.make_async_copy(src_hbm.at[pl.ds(tok, 1)], dst_buf.at[slot, pl.ds(j, 1)], sem.at[slot]).start()
        return carry
    lax.fori_loop(0, n_rows, body, 0)


def _row_gather_wait(slot, n_rows, src_hbm, dst_buf, sem):
    def body(j, carry):
        pltpu.make_async_copy(src_hbm.at[pl.ds(0, 1)], dst_buf.at[slot, pl.ds(j, 1)], sem.at[slot]).wait()
        return carry
    lax.fori_loop(0, n_rows, body, 0)


def _ffn_kernel(te_ref, nv_ref, src_ref, h_hbm, ws_ref, wg_ref, wu_ref, wd_ref, o_ref,
                idx_smem, xbuf, sem_i, sem_r):
    t = pl.program_id(0)
    nv = nv_ref[0]
    rows = xbuf.shape[1]
    slot = lax.rem(t, 2)

    def fetch(tt, s):
        cp = pltpu.make_async_copy(src_ref.at[tt], idx_smem.at[s], sem_i.at[s])
        cp.start()
        cp.wait()
        _row_gather_start(idx_smem, s, rows, h_hbm, xbuf, sem_r)

    @pl.when((t == 0) & (nv > 0))
    def _():
        fetch(0, 0)

    @pl.when(t + 1 < nv)
    def _():
        fetch(t + 1, 1 - slot)

    @pl.when(t < nv)
    def _():
        _row_gather_wait(slot, rows, h_hbm, xbuf, sem_r)
        x = xbuf[slot].astype(BF16)
        g = _dot(x, wg_ref[...])
        u = _dot(x, wu_ref[...])
        hm = (g * jax.nn.sigmoid(g) * u).astype(BF16)
        o_ref[...] = _dot(hm, wd_ref[...]) * ws_ref[...]

    @pl.when(t >= nv)
    def _():
        o_ref[...] = jnp.zeros_like(o_ref)


def _ffn(tile_e, n_valid, src, h2, wslot, wg, wu, wd):
    n_tiles, tmb = src.shape
    d = h2.shape[1]
    de = wg.shape[2]
    grid_spec = pltpu.PrefetchScalarGridSpec(
        num_scalar_prefetch=2,
        grid=(n_tiles,),
        in_specs=[
            pl.BlockSpec((n_tiles, tmb), lambda t, te, nv: (0, 0)),
            pl.BlockSpec(memory_space=pl.ANY),
            pl.BlockSpec((None, tmb, 1), lambda t, te, nv: (t, 0, 0)),
            pl.BlockSpec((None, d, de), lambda t, te, nv: (te[t], 0, 0)),
            pl.BlockSpec((None, d, de), lambda t, te, nv: (te[t], 0, 0)),
            pl.BlockSpec((None, de, d), lambda t, te, nv: (te[t], 0, 0)),
        ],
        out_specs=pl.BlockSpec((tmb, d), lambda t, te, nv: (t, 0)),
        scratch_shapes=[
            pltpu.SMEM((2, tmb), jnp.int32),
            pltpu.VMEM((2, tmb, d), F32),
            pltpu.SemaphoreType.DMA((2,)),
            pltpu.SemaphoreType.DMA((2,)),
        ],
    )
    return pl.pallas_call(
        _ffn_kernel,
        out_shape=jax.ShapeDtypeStruct((n_tiles * tmb, d), F32),
        grid_spec=grid_spec,
        compiler_params=pltpu.CompilerParams(
            dimension_semantics=("arbitrary",), vmem_limit_bytes=VMEM_LIMIT),
        name="ffn",
    )(tile_e, n_valid, src, h2, wslot.reshape(n_tiles, tmb, 1), wg, wu, wd)


def _combine_kernel(pos_ref, e_hbm, x_ref, g2_ref, lg_ref, lb_ref, o_ref, idx_smem, fbuf, sem_i, sem_r):
    t = pl.program_id(0)
    nt = pl.num_programs(0)
    tm = x_ref.shape[0]
    rows = fbuf.shape[1]
    slot = lax.rem(t, 2)

    def fetch(tt, s):
        cp = pltpu.make_async_copy(pos_ref.at[tt], idx_smem.at[s], sem_i.at[s])
        cp.start()
        cp.wait()
        _row_gather_start(idx_smem, s, rows, e_hbm, fbuf, sem_r)

    @pl.when(t == 0)
    def _():
        fetch(0, 0)

    @pl.when(t + 1 < nt)
    def _():
        fetch(t + 1, 1 - slot)

    _row_gather_wait(slot, rows, e_hbm, fbuf, sem_r)
    f = fbuf[slot, 0:tm, :] + fbuf[slot, tm:2 * tm, :]
    o_ref[...] = _layer_norm(ALPHA * x_ref[...] + g2_ref[...] * f, lg_ref[...], lb_ref[...])


def _combine(pos, eout, lat, mod4, ln_g, ln_b, n_batch, ctx_len, has_ctx):
    nb, lq, d = lat.shape
    tm = ROW_TILE
    ntb = lq // tm
    nct = ctx_len // tm if has_ctx else 0

    def row(t):
        return jnp.where(lax.rem(t, ntb) < nct, n_batch, t // ntb)

    return pl.pallas_call(
        _combine_kernel,
        out_shape=jax.ShapeDtypeStruct((nb * lq, d), F32),
        grid=(nb * ntb,),
        in_specs=[
            pl.BlockSpec(pos.shape, lambda t: (0, 0)),
            pl.BlockSpec(memory_space=pl.ANY),
            pl.BlockSpec((tm, d), lambda t: (t, 0)),
            pl.BlockSpec((None, None, 1, d), lambda t: (row(t), 5, 0, 0)),
            pl.BlockSpec(ln_g.shape, lambda t: (0, 0)),
            pl.BlockSpec(ln_b.shape, lambda t: (0, 0)),
        ],
        out_specs=pl.BlockSpec((tm, d), lambda t: (t, 0)),
        scratch_shapes=[
            pltpu.SMEM((2, TOP_K * tm), jnp.int32),
            pltpu.VMEM((2, TOP_K * tm, d), F32),
            pltpu.SemaphoreType.DMA((2,)),
            pltpu.SemaphoreType.DMA((2,)),
        ],
        compiler_params=pltpu.CompilerParams(
            dimension_semantics=("arbitrary",), vmem_limit_bytes=VMEM_LIMIT),
        name="combine",
    )(pos, eout, lat.reshape(nb * lq, d), mod4, ln_g, ln_b).reshape(nb, lq, d)


def _route(eid, ewt, tm):
    n_tok_tiles = eid.shape[0]
    n_tok = n_tok_tiles * tm
    n_assign = n_tok * TOP_K
    e_flat = eid.reshape(-1)
    tok = (jnp.arange(n_tok_tiles, dtype=jnp.int32)[:, None, None] * tm
           + jnp.arange(tm, dtype=jnp.int32)[None, None, :]
           + jnp.zeros((1, TOP_K, 1), jnp.int32)).reshape(-1)
    onehot = (e_flat[:, None] == jnp.arange(N_EXPERTS, dtype=jnp.int32)[None, :]).astype(jnp.int32)
    csum = jnp.cumsum(onehot, axis=0)
    rank = jnp.sum(onehot * csum, axis=1) - 1
    counts = csum[-1]
    tiles_per = (counts + MOE_TILE - 1) // MOE_TILE
    tile_end = jnp.cumsum(tiles_per)
    tile_start = tile_end - tiles_per
    dest = (tile_start[e_flat] * MOE_TILE + rank).astype(jnp.int32)
    n_tiles = -(-(n_assign + N_EXPERTS * (MOE_TILE - 1)) // MOE_TILE)
    tile_e = jnp.minimum(
        jnp.searchsorted(tile_end, jnp.arange(n_tiles, dtype=jnp.int32), side="right"), N_EXPERTS - 1
    ).astype(jnp.int32)
    n_valid = tile_end[-1:].astype(jnp.int32)
    src = jnp.zeros((n_tiles * MOE_TILE,), jnp.int32).at[dest].set(tok).reshape(n_tiles, MOE_TILE)
    wslot = jnp.zeros((n_tiles * MOE_TILE,), F32).at[dest].set(ewt.reshape(-1))
    pos = dest.reshape(n_tok_tiles, TOP_K * tm)
    return tile_e, n_valid, src, wslot, pos


def _swap_halves(a):
    h = a.shape[-1] // 2
    return jnp.concatenate([a[..., h:], a[..., :h]], axis=-1)


def _layer_weights(l, w_in, gm_ln_g, gm_ln_b, gm_ws, gm_bs, q_norm, w_uq, kv_norm, w_ukv, w_proj_a, w_proj_b,
                   b_merge, w_out, ln1_g, ln1_b):
    wi = w_in[l]
    kr = wi[:, OFF_KR:OFF_G]
    uq = w_uq[l].reshape(Q_LORA, N_HEADS, QK_NOPE + QK_ROPE)
    uq_r = uq[..., QK_NOPE:]
    uq_ext = jnp.concatenate([uq[..., :QK_NOPE], uq_r, _swap_halves(uq_r)], axis=-1)
    return dict(
        wu=wi[:, 0:GM_WIDTH].astype(BF16),
        wv=wi[:, GM_WIDTH:2 * GM_WIDTH].astype(BF16),
        wq=wi[:, OFF_Q:OFF_KV].astype(BF16),
        wkv=wi[:, OFF_KV:OFF_KR].astype(BF16),
        wkr=jnp.concatenate([kr, _swap_halves(kr)], axis=-1).astype(BF16),
        wg=wi[:, OFF_G:].astype(BF16),
        lng=gm_ln_g[l].reshape(1, -1), lnb=gm_ln_b[l].reshape(1, -1),
        ws=gm_ws[l].astype(BF16), bs=gm_bs[l].reshape(GM_GROUPS, CHUNK, 1),
        qn=q_norm[l].reshape(1, -1), wuq=uq_ext.reshape(Q_LORA, N_HEADS * QK_PAD).astype(BF16),
        kvn=kv_norm[l].reshape(1, -1), wukv=w_ukv[l].astype(BF16),
        bm=b_merge[l].reshape(1, -1),
        wa=w_proj_a[l].astype(BF16), wb=w_proj_b[l].astype(BF16), wo=w_out[l].astype(BF16),
        ln1g=ln1_g[l].reshape(1, -1), ln1b=ln1_b[l].reshape(1, -1),
    )


def _rope_table(seq, ctx_len):
    rows = seq // GRID_W
    r = jnp.repeat(jnp.arange(rows), GRID_W).astype(F32)
    c = jnp.tile(jnp.arange(GRID_W), rows).astype(F32)
    n_freq = QK_ROPE // 4
    inv = ROPE_THETA ** (-jnp.arange(n_freq, dtype=F32) / n_freq)
    ang = jnp.concatenate([r[:, None] * inv, c[:, None] * inv], axis=-1)
    cos, sin = jnp.cos(ang), jnp.sin(ang)
    lat = jnp.concatenate([cos, cos, -sin, sin], axis=-1)
    ctx = jnp.concatenate([jnp.ones((ctx_len, LANES // 2), F32), jnp.zeros((ctx_len, LANES // 2), F32)], axis=-1)
    return jnp.concatenate([ctx, lat], axis=0)


def kernel(x, c, ctx, c_ctx, w_ada, b_ada, w_in, gm_ln_g, gm_ln_b, gm_ws, gm_bs, q_norm, w_uq, kv_norm, w_ukv,
           w_proj_a, w_proj_b, b_merge, w_out, ln1_g, ln1_b, w_router, b_router, w_e_gate, w_e_up, w_e_down,
           ln2_g, ln2_b):
    nb, seq, d = x.shape
    ctx_len = ctx.shape[1]
    depth = w_ada.shape[0]
    tm = ROW_TILE

    mod_rows = -(-(nb + 1) // 8) * 8
    cc = jnp.concatenate([c, c_ctx[None, :], jnp.zeros((mod_rows - nb - 1, d), F32)], axis=0)
    mod = _ada(cc, w_ada, b_ada)
    tab = _rope_table(seq, ctx_len)
    wr_pad = jnp.concatenate([w_router, jnp.zeros((d, LANES - N_EXPERTS), F32)], axis=1)
    br_col = b_router.reshape(N_EXPERTS, 1)

    xc = jnp.concatenate([ctx, x], axis=1)
    for l in range(depth):
        last = l == depth - 1
        w = _layer_weights(l, w_in, gm_ln_g, gm_ln_b, gm_ws, gm_bs, q_norm, w_uq, kv_norm, w_ukv,
                           w_proj_a, w_proj_b, b_merge, w_out, ln1_g, ln1_b)
        mod4 = mod[l].reshape(mod_rows, 6, 1, d)
        gm, q, k, v, gate = _inproj(xc, mod4, w, tab, nb, ctx_len)
        att = _attention(q, k, v, ctx_len, latent_only=last)
        lat, h2, eid, ewt = _merge(xc, gm, att, gate, mod4, w, wr_pad, br_col, nb, ctx_len, latent_only=last)
        tile_e, n_valid, src, wslot, pos = _route(eid, ewt, tm)
        eout = _ffn(tile_e, n_valid, src, h2.reshape(-1, d), wslot,
                    w_e_gate[l].astype(BF16), w_e_up[l].astype(BF16), w_e_down[l].astype(BF16))
        xc = _combine(pos, eout, lat, mod4, ln2_g[l].reshape(1, -1), ln2_b[l].reshape(1, -1),
                      nb, ctx_len, has_ctx=not last)
    return xc
```

```python
import functools
import math

import jax
import jax.numpy as jnp
from jax import lax
from jax.experimental import pallas as pl
from jax.experimental.pallas import tpu as pltpu

F32 = jnp.float32
BF16 = jnp.bfloat16

D_MODEL = 1024
DEPTH = 2
GRID_W = 64
CHUNK = 128
GM_GROUPS = 8
GM_WIDTH = 1024
N_HEADS = 8
Q_LORA = 384
KV_LORA = 256
QK_NOPE = 128
QK_ROPE = 64
V_HEAD = 128
ROPE_THETA = 10000.0
N_EXPERTS = 16
N_EXPERT_GROUPS = 4
EXPERTS_PER_GROUP = N_EXPERTS // N_EXPERT_GROUPS
TOP_K = 2
D_EXPERT = 512
LN_EPS = 1e-5
ALPHA = (2.0 * DEPTH) ** 0.25

OFF_Q = 2 * GM_WIDTH
OFF_KV = OFF_Q + Q_LORA
OFF_KR = OFF_KV + KV_LORA
OFF_G = OFF_KR + QK_ROPE

LANES = 128
QK_PAD = 2 * LANES
V_PAD = 2 * LANES
ROW_TILE = 256
MOE_TILE = 512
DMA_UNROLL = 8
VMEM_LIMIT = 56 * 1024 * 1024

HIGHEST = lax.Precision.HIGHEST
LOG2E = 1.4426950408889634


def _dot(a, b):
    return jnp.dot(a, b, preferred_element_type=F32)


def _dot_nt(a, b):
    return lax.dot_general(a, b, (((1,), (1,)), ((), ())), preferred_element_type=F32)


def _layer_norm(z, g, b):
    mu = jnp.mean(z, axis=-1, keepdims=True)
    zc = z - mu
    var = jnp.mean(zc * zc, axis=-1, keepdims=True)
    return zc * lax.rsqrt(var + LN_EPS) * g + b


def _rms_norm(z, g):
    return z * lax.rsqrt(jnp.mean(z * z, axis=-1, keepdims=True) + LN_EPS) * g


def _params(*sem):
    return pltpu.CompilerParams(dimension_semantics=sem, vmem_limit_bytes=VMEM_LIMIT)


def _const_spec(shape):
    nd = len(shape)
    return pl.BlockSpec(shape, lambda *_: (0,) * nd, pipeline_mode=pl.Buffered(1))


def _mod_spec(row_fn, k, d):
    return pl.BlockSpec((None, None, 1, d), lambda b, i: (row_fn(b), k, 0, 0))


def _ada_kernel(c_ref, w_ref, b_ref, o_ref):
    cc = c_ref[...]
    s = cc * jax.nn.sigmoid(cc)
    o_ref[...] = jnp.dot(s, w_ref[...], precision=HIGHEST, preferred_element_type=F32) + b_ref[...]


def _ada(cc, w_ada, b_ada):
    depth, d, n = w_ada.shape
    rows = cc.shape[0]
    tn = 1536
    return pl.pallas_call(
        _ada_kernel,
        out_shape=jax.ShapeDtypeStruct((depth, rows, n), F32),
        grid=(depth, n // tn),
        in_specs=[
            pl.BlockSpec((rows, d), lambda l, j: (0, 0)),
            pl.BlockSpec((None, d, tn), lambda l, j: (l, 0, j)),
            pl.BlockSpec((None, 1, tn), lambda l, j: (l, 0, j)),
        ],
        out_specs=pl.BlockSpec((None, rows, tn), lambda l, j: (l, 0, j)),
        compiler_params=_params("parallel", "parallel"),
        name="ada",
    )(cc, w_ada, b_ada.reshape(depth, 1, n))


def _rope(r, tab):
    m = r * tab
    return m + pltpu.roll(m, LANES // 2, axis=1)


def _write_kv(h, tab, wkv_ref, wkr_ref, kvn_ref, wukv_ref, k_ref, v_ref):
    tm = h.shape[0]
    kvn = _rms_norm(_dot(h, wkv_ref[...]), kvn_ref[...]).astype(BF16)
    pkv = _dot(kvn, wukv_ref[...])
    kr = _dot(h, wkr_ref[...])
    if tab is not None:
        kr = _rope(kr, tab)
    lane = lax.broadcasted_iota(jnp.int32, (tm, LANES), 1)
    kr = jnp.where(lane < LANES // 2, kr, 0.0).astype(BF16)
    ones = jnp.ones((tm, V_PAD - V_HEAD), BF16)
    for hh in range(N_HEADS):
        o = hh * (QK_NOPE + V_HEAD)
        k_ref[hh, :, 0:QK_NOPE] = pkv[:, o:o + QK_NOPE].astype(BF16)
        k_ref[hh, :, QK_NOPE:QK_PAD] = kr
        v_ref[hh, :, 0:V_HEAD] = pkv[:, o + QK_NOPE:o + QK_NOPE + V_HEAD].astype(BF16)
        v_ref[hh, :, V_HEAD:V_PAD] = ones


def _inproj_kernel(*refs, use_rope):
    if use_rope:
        (x_ref, sc_ref, sh_ref, wu_ref, wv_ref, wq_ref, wkv_ref, wkr_ref, wg_ref, lng_ref, lnb_ref, ws_ref,
         bs_ref, qn_ref, wuq_ref, kvn_ref, wukv_ref, bm_ref, tab_ref, gm_ref, q_ref, k_ref, v_ref, gate_ref) = refs
        tab = tab_ref[...]
    else:
        (x_ref, sc_ref, sh_ref, wu_ref, wv_ref, wq_ref, wkv_ref, wkr_ref, wg_ref, lng_ref, lnb_ref, ws_ref,
         bs_ref, qn_ref, wuq_ref, kvn_ref, wukv_ref, bm_ref, gm_ref, q_ref, k_ref, v_ref, gate_ref) = refs
        tab = None
    tm = x_ref.shape[0]
    h = (x_ref[...] * (1.0 + sc_ref[...]) + sh_ref[...]).astype(BF16)

    u = jax.nn.gelu(_dot(h, wu_ref[...]))
    v = jax.nn.gelu(_dot(h, wv_ref[...]))
    vn = _layer_norm(v, lng_ref[...], lnb_ref[...]).astype(BF16)
    gd = GM_WIDTH // GM_GROUPS
    for c in range(tm // CHUNK):
        rs = slice(c * CHUNK, (c + 1) * CHUNK)
        for g in range(GM_GROUPS):
            cs = slice(g * gd, (g + 1) * gd)
            s = _dot(ws_ref[g], vn[rs, cs]) + bs_ref[g]
            gm_ref[rs, cs] = (u[rs, cs] * s).astype(BF16)

    scale = (QK_NOPE + QK_ROPE) ** -0.5 * LOG2E
    qn = _rms_norm(_dot(h, wq_ref[...]), qn_ref[...] * scale).astype(BF16)
    pq = _dot(qn, wuq_ref[...])
    for hh in range(N_HEADS):
        o = hh * QK_PAD
        q_ref[hh, :, 0:QK_NOPE] = pq[:, o:o + QK_NOPE].astype(BF16)
        qr = pq[:, o + QK_NOPE:o + QK_PAD]
        q_ref[hh, :, QK_NOPE:QK_PAD] = (qr if tab is None else _rope(qr, tab)).astype(BF16)

    _write_kv(h, tab, wkv_ref, wkr_ref, kvn_ref, wukv_ref, k_ref, v_ref)
    gate_ref[...] = jax.nn.sigmoid(_dot(h, wg_ref[...]) + bm_ref[...]).astype(BF16)


def _inproj(x, mod4, row_fn, w, tab, tm):
    nb, ln, d = x.shape
    weights = [w["wu"], w["wv"], w["wq"], w["wkv"], w["wkr"], w["wg"], w["lng"], w["lnb"], w["ws"],
               w["bs"], w["qn"], w["wuq"], w["kvn"], w["wukv"], w["bm"]]
    in_specs = [
        pl.BlockSpec((None, tm, d), lambda b, i: (b, i, 0)),
        _mod_spec(row_fn, 1, d), _mod_spec(row_fn, 0, d),
    ] + [_const_spec(a.shape) for a in weights]
    args = [x, mod4, mod4] + weights
    if tab is not None:
        in_specs.append(pl.BlockSpec((tm, LANES), lambda b, i: (i, 0)))
        args.append(tab)
    out_shape = (
        jax.ShapeDtypeStruct((nb, ln, GM_WIDTH), BF16),
        jax.ShapeDtypeStruct((nb, N_HEADS, ln, QK_PAD), BF16),
        jax.ShapeDtypeStruct((nb, N_HEADS, ln, QK_PAD), BF16),
        jax.ShapeDtypeStruct((nb, N_HEADS, ln, V_PAD), BF16),
        jax.ShapeDtypeStruct((nb, ln, 2 * d), BF16),
    )
    out_specs = (
        pl.BlockSpec((None, tm, GM_WIDTH), lambda b, i: (b, i, 0)),
        pl.BlockSpec((None, N_HEADS, tm, QK_PAD), lambda b, i: (b, 0, i, 0)),
        pl.BlockSpec((None, N_HEADS, tm, QK_PAD), lambda b, i: (b, 0, i, 0)),
        pl.BlockSpec((None, N_HEADS, tm, V_PAD), lambda b, i: (b, 0, i, 0)),
        pl.BlockSpec((None, tm, 2 * d), lambda b, i: (b, i, 0)),
    )
    return pl.pallas_call(
        functools.partial(_inproj_kernel, use_rope=tab is not None),
        out_shape=out_shape,
        grid=(nb, ln // tm),
        in_specs=in_specs,
        out_specs=out_specs,
        compiler_params=_params("parallel", "parallel"),
        name="inproj",
    )(*args)


def _ctx_kv_kernel(x_ref, sc_ref, sh_ref, wkv_ref, wkr_ref, kvn_ref, wukv_ref, k_ref, v_ref):
    h = (x_ref[...] * (1.0 + sc_ref[...]) + sh_ref[...]).astype(BF16)
    _write_kv(h, None, wkv_ref, wkr_ref, kvn_ref, wukv_ref, k_ref, v_ref)


def _ctx_kv(x, mod4, row_fn, w, tm):
    nb, ln, d = x.shape
    weights = [w["wkv"], w["wkr"], w["kvn"], w["wukv"]]
    return pl.pallas_call(
        _ctx_kv_kernel,
        out_shape=(jax.ShapeDtypeStruct((nb, N_HEADS, ln, QK_PAD), BF16),
                   jax.ShapeDtypeStruct((nb, N_HEADS, ln, V_PAD), BF16)),
        grid=(nb, ln // tm),
        in_specs=[pl.BlockSpec((None, tm, d), lambda b, i: (b, i, 0)),
                  _mod_spec(row_fn, 1, d), _mod_spec(row_fn, 0, d)] + [_const_spec(a.shape) for a in weights],
        out_specs=(pl.BlockSpec((None, N_HEADS, tm, QK_PAD), lambda b, i: (b, 0, i, 0)),
                   pl.BlockSpec((None, N_HEADS, tm, V_PAD), lambda b, i: (b, 0, i, 0))),
        compiler_params=_params("parallel", "parallel"),
        name="ctx_kv",
    )(x, mod4, mod4, *weights)


def _attn_kernel(*refs, n_kv, sub):
    q_ref, o_ref = refs[0], refs[-1]
    kv = [(refs[1 + 2 * j], refs[2 + 2 * j]) for j in range(n_kv)]
    for r0 in range(0, q_ref.shape[0], sub):
        q = q_ref[r0:r0 + sub, :]
        s = [_dot_nt(q, k_ref[...]) for k_ref, _ in kv]
        m = s[0].max(axis=-1, keepdims=True)
        for sj in s[1:]:
            m = jnp.maximum(m, sj.max(axis=-1, keepdims=True))
        acc = None
        for sj, (_, v_ref) in zip(s, kv):
            p = jnp.exp2((sj - m).astype(BF16))
            pv = _dot(p, v_ref[...])
            acc = pv if acc is None else acc + pv
        o_ref[r0:r0 + sub, :] = (acc[:, 0:V_HEAD] / acc[:, V_HEAD:V_PAD]).astype(BF16)


def _attention(q, kvs, tq):
    nb, nh, lq, _ = q.shape
    in_specs = [pl.BlockSpec((None, None, tq, QK_PAD), lambda b, h, i: (b, h, i, 0))]
    args = [q]
    for k, v in kvs:
        lk = k.shape[2]
        in_specs.append(pl.BlockSpec((None, None, lk, QK_PAD), lambda b, h, i: (b, h, 0, 0)))
        in_specs.append(pl.BlockSpec((None, None, lk, V_PAD), lambda b, h, i: (b, h, 0, 0)))
        args += [k, v]
    return pl.pallas_call(
        functools.partial(_attn_kernel, n_kv=len(kvs), sub=min(tq, ROW_TILE)),
        out_shape=jax.ShapeDtypeStruct((nb, lq, nh * V_HEAD), BF16),
        grid=(nb, nh, lq // tq),
        in_specs=in_specs,
        out_specs=pl.BlockSpec((None, tq, V_HEAD), lambda b, h, i: (b, i, h)),
        compiler_params=_params("parallel", "parallel", "parallel"),
        name="attn",
    )(*args)


def _top2_of4(a, b, c, d):
    m1, n1 = jnp.maximum(a, b), jnp.minimum(a, b)
    m2, n2 = jnp.maximum(c, d), jnp.minimum(c, d)
    return jnp.maximum(m1, m2) + jnp.maximum(jnp.minimum(m1, m2), jnp.maximum(n1, n2))


def _merge_kernel(x_ref, gm_ref, att_ref, gate_ref, wa_ref, wb_ref, wo_ref, g1_ref, lg_ref, lb_ref,
                  sc2_ref, sh2_ref, wr_ref, br_ref, lat_ref, h2_ref, eid_ref, ewt_ref):
    d = x_ref.shape[1]
    ya = _dot(gm_ref[...], wa_ref[...])
    yb = _dot(att_ref[...], wb_ref[...])
    y = gate_ref[:, 0:d].astype(F32) * ya + gate_ref[:, d:2 * d].astype(F32) * yb
    mix = _dot(y.astype(BF16), wo_ref[...])
    lat = _layer_norm(ALPHA * x_ref[...] + g1_ref[...] * mix, lg_ref[...], lb_ref[...])
    lat_ref[...] = lat
    h2 = lat * (1.0 + sc2_ref[...]) + sh2_ref[...]
    h2_ref[...] = h2

    logits = jnp.dot(h2, wr_ref[...], precision=HIGHEST, preferred_element_type=F32)
    aff_t = jax.nn.sigmoid(logits.T[0:N_EXPERTS, :])
    sel_t = aff_t + br_ref[...]
    aff = [aff_t[e:e + 1, :] for e in range(N_EXPERTS)]
    sel = [sel_t[e:e + 1, :] for e in range(N_EXPERTS)]
    epg = EXPERTS_PER_GROUP
    best = _top2_of4(*sel[0:epg])
    grp = jnp.zeros_like(best, dtype=jnp.int32)
    for g in range(1, N_EXPERT_GROUPS):
        sc = _top2_of4(*sel[g * epg:(g + 1) * epg])
        better = sc > best
        best = jnp.where(better, sc, best)
        grp = jnp.where(better, g, grp)

    def pick(vals, j):
        out = vals[j]
        for g in range(1, N_EXPERT_GROUPS):
            out = jnp.where(grp == g, vals[g * epg + j], out)
        return out

    s_in = [pick(sel, j) for j in range(epg)]
    a_in = [pick(aff, j) for j in range(epg)]
    i1 = jnp.zeros_like(grp)
    v1, w1 = s_in[0], a_in[0]
    for j in range(1, epg):
        better = s_in[j] > v1
        v1 = jnp.where(better, s_in[j], v1)
        w1 = jnp.where(better, a_in[j], w1)
        i1 = jnp.where(better, j, i1)
    i2 = jnp.full_like(grp, -1)
    v2 = jnp.full_like(v1, -jnp.inf)
    w2 = jnp.zeros_like(w1)
    for j in range(epg):
        better = (i1 != j) & (s_in[j] > v2)
        v2 = jnp.where(better, s_in[j], v2)
        w2 = jnp.where(better, a_in[j], w2)
        i2 = jnp.where(better, j, i2)
    tot = w1 + w2
    eid_ref[0:1, :] = grp * epg + i1
    eid_ref[1:2, :] = grp * epg + i2
    ewt_ref[0:1, :] = w1 / tot
    ewt_ref[1:2, :] = w2 / tot


def _merge(x, gm, att, gate, mod4, row_fn, w, wr_pad, br_col, tm):
    nb, ln, d = x.shape
    nt = ln // tm
    weights = [w["wa"], w["wb"], w["wo"]]

    def tile3(width):
        return pl.BlockSpec((None, tm, width), lambda b, i: (b, i, 0))

    in_specs = [tile3(d), tile3(GM_WIDTH), tile3(N_HEADS * V_HEAD), tile3(2 * d)] + [
        _const_spec(a.shape) for a in weights] + [
        _mod_spec(row_fn, 2, d), _const_spec(w["ln1g"].shape), _const_spec(w["ln1b"].shape),
        _mod_spec(row_fn, 4, d), _mod_spec(row_fn, 3, d), _const_spec(wr_pad.shape), _const_spec(br_col.shape)]
    out_shape = (
        jax.ShapeDtypeStruct((nb, ln, d), F32),
        jax.ShapeDtypeStruct((nb, ln, d), F32),
        jax.ShapeDtypeStruct((nb * nt, TOP_K, tm), jnp.int32),
        jax.ShapeDtypeStruct((nb * nt, TOP_K, tm), F32),
    )
    out_specs = (
        tile3(d), tile3(d),
        pl.BlockSpec((None, TOP_K, tm), lambda b, i: (b * nt + i, 0, 0)),
        pl.BlockSpec((None, TOP_K, tm), lambda b, i: (b * nt + i, 0, 0)),
    )
    return pl.pallas_call(
        _merge_kernel,
        out_shape=out_shape,
        grid=(nb, nt),
        in_specs=in_specs,
        out_specs=out_specs,
        compiler_params=_params("parallel", "parallel"),
        name="merge",
    )(x, gm, att, gate, *weights, mod4, w["ln1g"], w["ln1b"], mod4, mod4, wr_pad, br_col)


def _load_indices(pos_ref, t, idx_smem, sem):
    cp = pltpu.make_async_copy(pos_ref.at[t], idx_smem, sem)
    cp.start()
    cp.wait()


def _dispatch_kernel(meta_ref, pos_ref, h_ref, buf_hbm, idx_smem, zbuf, sem_i, sem_r, sem_z):
    tm = h_ref.shape[0]
    tmb = zbuf.shape[0]
    n_tiles = buf_hbm.shape[0] // tmb

    @pl.when(pl.program_id(0) == 0)
    def _():
        zbuf[...] = jnp.zeros_like(zbuf)
        n_valid = meta_ref[2 * N_EXPERTS]
        fills = [(meta_ref[N_EXPERTS + e] > 0, meta_ref[e]) for e in range(N_EXPERTS)]
        fills += [(n_valid + j < n_tiles, n_valid + j) for j in range(N_EXPERTS + 1)]
        copies = [(need, pltpu.make_async_copy(zbuf, buf_hbm.at[pl.ds(pl.multiple_of(tile * tmb, tmb), tmb)], sem_z))
                  for need, tile in fills]
        for need, cp in copies:
            pl.when(need)(cp.start)
        for need, cp in copies:
            pl.when(need)(cp.wait)

    _load_indices(pos_ref, pl.program_id(0), idx_smem, sem_i)

    def body(j, carry):
        for k in range(TOP_K):
            dst = idx_smem[k * tm + j]
            pltpu.make_async_copy(h_ref.at[pl.ds(j, 1)], buf_hbm.at[pl.ds(dst, 1)], sem_r).start()
        return carry

    lax.fori_loop(0, tm, body, 0, unroll=DMA_UNROLL)
    for k in range(TOP_K):
        pltpu.make_async_copy(h_ref, buf_hbm.at[pl.ds(0, tm)], sem_r).wait()


def _dispatch(meta, pos, h2, n_rows, tm):
    n_tok, d = h2.shape
    grid_spec = pltpu.PrefetchScalarGridSpec(
        num_scalar_prefetch=1,
        grid=(n_tok // tm,),
        in_specs=[_const_spec(pos.shape), pl.BlockSpec((tm, d), lambda t, meta: (t, 0))],
        out_specs=pl.BlockSpec(memory_space=pl.ANY),
        scratch_shapes=[pltpu.SMEM((TOP_K * tm,), jnp.int32), pltpu.VMEM((MOE_TILE, d), F32),
                        pltpu.SemaphoreType.DMA, pltpu.SemaphoreType.DMA, pltpu.SemaphoreType.DMA],
    )
    return pl.pallas_call(
        _dispatch_kernel,
        out_shape=jax.ShapeDtypeStruct((n_rows, d), F32),
        grid_spec=grid_spec,
        compiler_params=_params("arbitrary"),
        name="dispatch",
    )(meta, pos, h2)


def _ffn_kernel(te_ref, nr_ref, nv_ref, x_ref, wg_ref, wu_ref, wd_ref, o_ref):
    nr = nr_ref[pl.program_id(0)]

    @pl.when(nr > 0)
    def _():
        row = lax.broadcasted_iota(jnp.int32, x_ref.shape, 0)
        x = jnp.where(row < nr, x_ref[...], 0.0).astype(BF16)
        g = _dot(x, wg_ref[...])
        u = _dot(x, wu_ref[...])
        hm = (g * jax.nn.sigmoid(g) * u).astype(BF16)
        o_ref[...] = _dot(hm, wd_ref[...])

    @pl.when(nr == 0)
    def _():
        o_ref[...] = jnp.zeros_like(o_ref)


def _ffn(tile_e, tile_rows, n_valid, buf, wg, wu, wd):
    n_tiles = tile_e.shape[0]
    tmb = buf.shape[0] // n_tiles
    d = buf.shape[1]
    de = wg.shape[2]

    def xmap(t, te, nr, nv):
        return (jnp.minimum(t, nv[0] - 1), 0)

    grid_spec = pltpu.PrefetchScalarGridSpec(
        num_scalar_prefetch=3,
        grid=(n_tiles,),
        in_specs=[
            pl.BlockSpec((tmb, d), xmap),
            pl.BlockSpec((None, d, de), lambda t, te, nr, nv: (te[t], 0, 0)),
            pl.BlockSpec((None, d, de), lambda t, te, nr, nv: (te[t], 0, 0)),
            pl.BlockSpec((None, de, d), lambda t, te, nr, nv: (te[t], 0, 0)),
        ],
        out_specs=pl.BlockSpec((tmb, d), lambda t, te, nr, nv: (t, 0)),
    )
    return pl.pallas_call(
        _ffn_kernel,
        out_shape=jax.ShapeDtypeStruct((n_tiles * tmb, d), F32),
        grid_spec=grid_spec,
        compiler_params=_params("arbitrary"),
        name="ffn",
    )(tile_e, tile_rows, n_valid, buf, wg, wu, wd)


def _combine_kernel(pos_ref, e_hbm, x_ref, wt_ref, g2_ref, lg_ref, lb_ref, o_ref, idx_smem, fbuf, sem_i, sem_r):
    tm = x_ref.shape[0]
    rows = fbuf.shape[0]
    _load_indices(pos_ref, pl.program_id(0) * pl.num_programs(1) + pl.program_id(1), idx_smem, sem_i)

    def body(j, carry):
        src = idx_smem[j]
        pltpu.make_async_copy(e_hbm.at[pl.ds(src, 1)], fbuf.at[pl.ds(j, 1)], sem_r).start()
        return carry

    lax.fori_loop(0, rows, body, 0, unroll=DMA_UNROLL)
    pltpu.make_async_copy(e_hbm.at[pl.ds(0, rows)], fbuf, sem_r).wait()
    f = wt_ref[:, 0:1] * fbuf[0:tm, :] + wt_ref[:, 1:2] * fbuf[tm:2 * tm, :]
    o_ref[...] = _layer_norm(ALPHA * x_ref[...] + g2_ref[...] * f, lg_ref[...], lb_ref[...])


def _combine(pos, eout, wt, lat, mod4, row_fn, ln_g, ln_b, tm):
    nb, ln, d = lat.shape
    ntb = ln // tm
    return pl.pallas_call(
        _combine_kernel,
        out_shape=jax.ShapeDtypeStruct((nb, ln, d), F32),
        grid=(nb, ntb),
        in_specs=[
            _const_spec(pos.shape),
            pl.BlockSpec(memory_space=pl.ANY),
            pl.BlockSpec((None, tm, d), lambda b, i: (b, i, 0)),
            pl.BlockSpec((None, tm, TOP_K), lambda b, i: (b * ntb + i, 0, 0)),
            _mod_spec(row_fn, 5, d), _const_spec(ln_g.shape), _const_spec(ln_b.shape),
        ],
        out_specs=pl.BlockSpec((None, tm, d), lambda b, i: (b, i, 0)),
        scratch_shapes=[
            pltpu.SMEM((TOP_K * tm,), jnp.int32),
            pltpu.VMEM((TOP_K * tm, d), F32),
            pltpu.SemaphoreType.DMA,
            pltpu.SemaphoreType.DMA,
        ],
        compiler_params=_params("arbitrary", "arbitrary"),
        name="combine",
    )(pos, eout, lat, wt, mod4, ln_g, ln_b)


def _route(eid):
    n_tok_tiles, _, tm = eid.shape
    n_assign = n_tok_tiles * tm * TOP_K
    e_flat = eid.reshape(-1)
    experts = jnp.arange(N_EXPERTS, dtype=jnp.int32)
    onehot = (e_flat[:, None] == experts[None, :]).astype(jnp.int32)
    csum = jnp.cumsum(onehot, axis=0)
    rank = jnp.sum(onehot * csum, axis=1) - 1
    counts = csum[-1]
    tiles_per = (counts + MOE_TILE - 1) // MOE_TILE
    tile_end = jnp.cumsum(tiles_per)
    tile_start = tile_end - tiles_per
    pos = (jnp.sum(onehot * tile_start[None, :], axis=1) * MOE_TILE + rank).astype(jnp.int32)
    n_tiles = -(-(n_assign + N_EXPERTS * (MOE_TILE - 1)) // MOE_TILE)
    t_idx = jnp.arange(n_tiles, dtype=jnp.int32)
    tile_e = jnp.minimum(jnp.sum((t_idx[:, None] >= tile_end[None, :]).astype(jnp.int32), axis=1), N_EXPERTS - 1)
    own = (tile_e[:, None] == experts[None, :]).astype(jnp.int32)
    left = jnp.sum(own * (counts - (t_idx[:, None] - tile_start[None, :]) * MOE_TILE), axis=1)
    tile_rows = jnp.where(t_idx < tile_end[-1], jnp.clip(left, 0, MOE_TILE), 0).astype(jnp.int32)
    n_valid = tile_end[-1:].astype(jnp.int32)
    meta = jnp.concatenate([tile_end - 1, tiles_per, n_valid]).astype(jnp.int32)
    return tile_e.astype(jnp.int32), tile_rows, n_valid, meta, pos.reshape(n_tok_tiles, TOP_K * tm), n_tiles


def _moe(lat, h2, eid, ewt, mod4, row_fn, wg, wu, wd, ln_g, ln_b, tm):
    d = lat.shape[-1]
    tile_e, tile_rows, n_valid, meta, pos, n_tiles = _route(eid)
    buf = _dispatch(meta, pos, h2.reshape(-1, d), n_tiles * MOE_TILE, tm)
    eout = _ffn(tile_e, tile_rows, n_valid, buf, wg, wu, wd)
    return _combine(pos, eout, ewt.transpose(0, 2, 1), lat, mod4, row_fn, ln_g, ln_b, tm)


def _swap_halves(a):
    h = a.shape[-1] // 2
    return jnp.concatenate([a[..., h:], a[..., :h]], axis=-1)


def _layer_weights(l, w_in, gm_ln_g, gm_ln_b, gm_ws, gm_bs, q_norm, w_uq, kv_norm, w_ukv, w_proj_a, w_proj_b,
                   b_merge, w_out, ln1_g, ln1_b):
    wi = w_in[l]
    kr = wi[:, OFF_KR:OFF_G]
    uq = w_uq[l].reshape(Q_LORA, N_HEADS, QK_NOPE + QK_ROPE)
    uq_r = uq[..., QK_NOPE:]
    uq_ext = jnp.concatenate([uq[..., :QK_NOPE], uq_r, _swap_halves(uq_r)], axis=-1)
    return dict(
        wu=wi[:, 0:GM_WIDTH].astype(BF16),
        wv=wi[:, GM_WIDTH:2 * GM_WIDTH].astype(BF16),
        wq=wi[:, OFF_Q:OFF_KV].astype(BF16),
        wkv=wi[:, OFF_KV:OFF_KR].astype(BF16),
        wkr=jnp.concatenate([kr, _swap_halves(kr)], axis=-1).astype(BF16),
        wg=wi[:, OFF_G:].astype(BF16),
        lng=gm_ln_g[l].reshape(1, -1), lnb=gm_ln_b[l].reshape(1, -1),
        ws=gm_ws[l].astype(BF16), bs=gm_bs[l].reshape(GM_GROUPS, CHUNK, 1),
        qn=q_norm[l].reshape(1, -1), wuq=uq_ext.reshape(Q_LORA, N_HEADS * QK_PAD).astype(BF16),
        kvn=kv_norm[l].reshape(1, -1), wukv=w_ukv[l].astype(BF16),
        bm=b_merge[l].reshape(1, -1),
        wa=w_proj_a[l].astype(BF16), wb=w_proj_b[l].astype(BF16), wo=w_out[l].astype(BF16),
        ln1g=ln1_g[l].reshape(1, -1), ln1b=ln1_b[l].reshape(1, -1),
    )


def _rope_table(seq):
    rows = seq // GRID_W
    r = jnp.repeat(jnp.arange(rows), GRID_W).astype(F32)
    c = jnp.tile(jnp.arange(GRID_W), rows).astype(F32)
    n_freq = QK_ROPE // 4
    inv = ROPE_THETA ** (-jnp.arange(n_freq, dtype=F32) / n_freq)
    ang = jnp.concatenate([r[:, None] * inv, c[:, None] * inv], axis=-1)
    cos, sin = jnp.cos(ang), jnp.sin(ang)
    return jnp.concatenate([cos, cos, -sin, sin], axis=-1)


def kernel(x, c, ctx, c_ctx, w_ada, b_ada, w_in, gm_ln_g, gm_ln_b, gm_ws, gm_bs, q_norm, w_uq, kv_norm, w_ukv,
           w_proj_a, w_proj_b, b_merge, w_out, ln1_g, ln1_b, w_router, b_router, w_e_gate, w_e_up, w_e_down,
           ln2_g, ln2_b):
    nb, seq, d = x.shape
    depth = w_ada.shape[0]
    tm = ROW_TILE
    tq = 2 * ROW_TILE if seq % (2 * ROW_TILE) == 0 else ROW_TILE

    mod_rows = -(-(nb + 1) // 8) * 8
    cc = jnp.concatenate([c, c_ctx[None, :], jnp.zeros((mod_rows - nb - 1, d), F32)], axis=0)
    mod = _ada(cc, w_ada, b_ada)
    tab = _rope_table(seq)
    wr_pad = jnp.concatenate([w_router, jnp.zeros((d, LANES - N_EXPERTS), F32)], axis=1)
    br_col = b_router.reshape(N_EXPERTS, 1)

    def lat_row(b):
        return b

    def ctx_row(b):
        return nb

    lat, cx = x, ctx
    for l in range(depth):
        last = l == depth - 1
        w = _layer_weights(l, w_in, gm_ln_g, gm_ln_b, gm_ws, gm_bs, q_norm, w_uq, kv_norm, w_ukv,
                           w_proj_a, w_proj_b, b_merge, w_out, ln1_g, ln1_b)
        wg, wu, wd = w_e_gate[l].astype(BF16), w_e_up[l].astype(BF16), w_e_down[l].astype(BF16)
        ln2 = (ln2_g[l].reshape(1, -1), ln2_b[l].reshape(1, -1))
        mod4 = mod[l].reshape(mod_rows, 6, 1, d)

        if last:
            ck, cv = _ctx_kv(cx, mod4, ctx_row, w, tm)
        else:
            cgm, cq, ck, cv, cgate = _inproj(cx, mod4, ctx_row, w, None, tm)
        gm, q, k, v, gate = _inproj(lat, mod4, lat_row, w, tab, tq)
        att = _attention(q, [(ck, cv), (k, v)], tq)
        lat1, h2, eid, ewt = _merge(lat, gm, att, gate, mod4, lat_row, w, wr_pad, br_col, tq)
        lat = _moe(lat1, h2, eid, ewt, mod4, lat_row, wg, wu, wd, *ln2, tq)
        if not last:
            catt = _attention(cq, [(ck, cv)], tm)
            cx1, ch2, ceid, cewt = _merge(cx, cgm, catt, cgate, mod4, ctx_row, w, wr_pad, br_col, tm)
            cx = _moe(cx1, ch2, ceid, cewt, mod4, ctx_row, wg, wu, wd, *ln2, tm)
    return lat
```

```python
import functools
import math

import jax
import jax.numpy as jnp
from jax import lax
from jax.experimental import pallas as pl
from jax.experimental.pallas import tpu as pltpu

F32 = jnp.float32
BF16 = jnp.bfloat16

D_MODEL = 1024
DEPTH = 2
GRID_W = 64
CHUNK = 128
GM_GROUPS = 8
GM_WIDTH = 1024
N_HEADS = 8
Q_LORA = 384
KV_LORA = 256
QK_NOPE = 128
QK_ROPE = 64
V_HEAD = 128
ROPE_THETA = 10000.0
N_EXPERTS = 16
N_EXPERT_GROUPS = 4
EXPERTS_PER_GROUP = N_EXPERTS // N_EXPERT_GROUPS
TOP_K = 2
D_EXPERT = 512
LN_EPS = 1e-5
ALPHA = (2.0 * DEPTH) ** 0.25

OFF_Q = 2 * GM_WIDTH
OFF_KV = OFF_Q + Q_LORA
OFF_KR = OFF_KV + KV_LORA
OFF_G = OFF_KR + QK_ROPE

LANES = 128
QK_PAD = 2 * LANES
V_PAD = 2 * LANES
ROW_TILE = 256
MOE_TILE = 512
DMA_UNROLL = 8
SUBLANES = 8
VMEM_LIMIT = 56 * 1024 * 1024

HIGHEST = lax.Precision.HIGHEST
LOG2E = 1.4426950408889634


def _dot(a, b):
    return jnp.dot(a, b, preferred_element_type=F32)


def _dot_nt(a, b):
    return lax.dot_general(a, b, (((1,), (1,)), ((), ())), preferred_element_type=F32)


def _layer_norm(z, g, b):
    mu = jnp.mean(z, axis=-1, keepdims=True)
    zc = z - mu
    var = jnp.mean(zc * zc, axis=-1, keepdims=True)
    return zc * lax.rsqrt(var + LN_EPS) * g + b


def _rms_norm(z, g):
    return z * lax.rsqrt(jnp.mean(z * z, axis=-1, keepdims=True) + LN_EPS) * g


def _to_row_tiles(ref, val, row0=0):
    rows, d = val.shape
    for c in range(d // LANES):
        ref[pl.ds(row0 * SUBLANES + c, rows, stride=SUBLANES), :] = val[:, c * LANES:(c + 1) * LANES]


def _from_row_tiles(ref, rows, row0=0):
    return jnp.concatenate(
        [ref[pl.ds(row0 * SUBLANES + c, rows, stride=SUBLANES), :] for c in range(SUBLANES)], axis=-1)


def _row_tile(ref, r):
    return ref.at[pl.ds(pl.multiple_of(r * SUBLANES, SUBLANES), SUBLANES)]


def _params(*sem):
    return pltpu.CompilerParams(dimension_semantics=sem, vmem_limit_bytes=VMEM_LIMIT)


def _const_spec(shape):
    nd = len(shape)
    return pl.BlockSpec(shape, lambda *_: (0,) * nd, pipeline_mode=pl.Buffered(1))


def _mod_spec(row_fn, k, d):
    return pl.BlockSpec((None, None, 1, d), lambda b, i: (row_fn(b), k, 0, 0))


def _ada_kernel(c_ref, w_ref, b_ref, o_ref):
    cc = c_ref[...]
    s = cc * jax.nn.sigmoid(cc)
    o_ref[...] = jnp.dot(s, w_ref[...], precision=HIGHEST, preferred_element_type=F32) + b_ref[...]


def _ada(cc, w_ada, b_ada):
    depth, d, n = w_ada.shape
    rows = cc.shape[0]
    tn = 1536
    return pl.pallas_call(
        _ada_kernel,
        out_shape=jax.ShapeDtypeStruct((depth, rows, n), F32),
        grid=(depth, n // tn),
        in_specs=[
            pl.BlockSpec((rows, d), lambda l, j: (0, 0)),
            pl.BlockSpec((None, d, tn), lambda l, j: (l, 0, j)),
            pl.BlockSpec((None, 1, tn), lambda l, j: (l, 0, j)),
        ],
        out_specs=pl.BlockSpec((None, rows, tn), lambda l, j: (l, 0, j)),
        compiler_params=_params("parallel", "parallel"),
        name="ada",
    )(cc, w_ada, b_ada.reshape(depth, 1, n))


def _rope(r, tab):
    m = r * tab
    return m + pltpu.roll(m, LANES // 2, axis=1)


def _write_kv(h, tab, wkv_ref, wkr_ref, kvn_ref, wukv_ref, k_ref, v_ref):
    tm = h.shape[0]
    kvn = _rms_norm(_dot(h, wkv_ref[...]), kvn_ref[...]).astype(BF16)
    pkv = _dot(kvn, wukv_ref[...])
    kr = _dot(h, wkr_ref[...])
    if tab is not None:
        kr = _rope(kr, tab)
    lane = lax.broadcasted_iota(jnp.int32, (tm, LANES), 1)
    kr = jnp.where(lane < LANES // 2, kr, 0.0).astype(BF16)
    ones = jnp.ones((tm, V_PAD - V_HEAD), BF16)
    for hh in range(N_HEADS):
        o = hh * (QK_NOPE + V_HEAD)
        k_ref[hh, :, 0:QK_NOPE] = pkv[:, o:o + QK_NOPE].astype(BF16)
        k_ref[hh, :, QK_NOPE:QK_PAD] = kr
        v_ref[hh, :, 0:V_HEAD] = pkv[:, o + QK_NOPE:o + QK_NOPE + V_HEAD].astype(BF16)
        v_ref[hh, :, V_HEAD:V_PAD] = ones


def _inproj_kernel(*refs, use_rope):
    if use_rope:
        (x_ref, sc_ref, sh_ref, wu_ref, wv_ref, wq_ref, wkv_ref, wkr_ref, wg_ref, lng_ref, lnb_ref, ws_ref,
         bs_ref, qn_ref, wuq_ref, kvn_ref, wukv_ref, bm_ref, tab_ref, gm_ref, q_ref, k_ref, v_ref, gate_ref) = refs
        tab = tab_ref[...]
    else:
        (x_ref, sc_ref, sh_ref, wu_ref, wv_ref, wq_ref, wkv_ref, wkr_ref, wg_ref, lng_ref, lnb_ref, ws_ref,
         bs_ref, qn_ref, wuq_ref, kvn_ref, wukv_ref, bm_ref, gm_ref, q_ref, k_ref, v_ref, gate_ref) = refs
        tab = None
    tm = x_ref.shape[0]
    h = (x_ref[...] * (1.0 + sc_ref[...]) + sh_ref[...]).astype(BF16)

    u = jax.nn.gelu(_dot(h, wu_ref[...]))
    v = jax.nn.gelu(_dot(h, wv_ref[...]))
    vn = _layer_norm(v, lng_ref[...], lnb_ref[...]).astype(BF16)
    gd = GM_WIDTH // GM_GROUPS
    for c in range(tm // CHUNK):
        rs = slice(c * CHUNK, (c + 1) * CHUNK)
        for g in range(GM_GROUPS):
            cs = slice(g * gd, (g + 1) * gd)
            s = _dot(ws_ref[g], vn[rs, cs]) + bs_ref[g]
            gm_ref[rs, cs] = (u[rs, cs] * s).astype(BF16)

    scale = (QK_NOPE + QK_ROPE) ** -0.5 * LOG2E
    qn = _rms_norm(_dot(h, wq_ref[...]), qn_ref[...] * scale).astype(BF16)
    pq = _dot(qn, wuq_ref[...])
    for hh in range(N_HEADS):
        o = hh * QK_PAD
        q_ref[hh, :, 0:QK_NOPE] = pq[:, o:o + QK_NOPE].astype(BF16)
        qr = pq[:, o + QK_NOPE:o + QK_PAD]
        q_ref[hh, :, QK_NOPE:QK_PAD] = (qr if tab is None else _rope(qr, tab)).astype(BF16)

    _write_kv(h, tab, wkv_ref, wkr_ref, kvn_ref, wukv_ref, k_ref, v_ref)
    gate_ref[...] = jax.nn.sigmoid(_dot(h, wg_ref[...]) + bm_ref[...]).astype(BF16)


def _inproj(x, mod4, row_fn, w, tab, tm):
    nb, ln, d = x.shape
    weights = [w["wu"], w["wv"], w["wq"], w["wkv"], w["wkr"], w["wg"], w["lng"], w["lnb"], w["ws"],
               w["bs"], w["qn"], w["wuq"], w["kvn"], w["wukv"], w["bm"]]
    in_specs = [
        pl.BlockSpec((None, tm, d), lambda b, i: (b, i, 0)),
        _mod_spec(row_fn, 1, d), _mod_spec(row_fn, 0, d),
    ] + [_const_spec(a.shape) for a in weights]
    args = [x, mod4, mod4] + weights
    if tab is not None:
        in_specs.append(pl.BlockSpec((tm, LANES), lambda b, i: (i, 0)))
        args.append(tab)
    out_shape = (
        jax.ShapeDtypeStruct((nb, ln, GM_WIDTH), BF16),
        jax.ShapeDtypeStruct((nb, N_HEADS, ln, QK_PAD), BF16),
        jax.ShapeDtypeStruct((nb, N_HEADS, ln, QK_PAD), BF16),
        jax.ShapeDtypeStruct((nb, N_HEADS, ln, V_PAD), BF16),
        jax.ShapeDtypeStruct((nb, ln, 2 * d), BF16),
    )
    out_specs = (
        pl.BlockSpec((None, tm, GM_WIDTH), lambda b, i: (b, i, 0)),
        pl.BlockSpec((None, N_HEADS, tm, QK_PAD), lambda b, i: (b, 0, i, 0)),
        pl.BlockSpec((None, N_HEADS, tm, QK_PAD), lambda b, i: (b, 0, i, 0)),
        pl.BlockSpec((None, N_HEADS, tm, V_PAD), lambda b, i: (b, 0, i, 0)),
        pl.BlockSpec((None, tm, 2 * d), lambda b, i: (b, i, 0)),
    )
    return pl.pallas_call(
        functools.partial(_inproj_kernel, use_rope=tab is not None),
        out_shape=out_shape,
        grid=(nb, ln // tm),
        in_specs=in_specs,
        out_specs=out_specs,
        compiler_params=_params("parallel", "parallel"),
        name="inproj",
    )(*args)


def _ctx_kv_kernel(x_ref, sc_ref, sh_ref, wkv_ref, wkr_ref, kvn_ref, wukv_ref, k_ref, v_ref):
    h = (x_ref[...] * (1.0 + sc_ref[...]) + sh_ref[...]).astype(BF16)
    _write_kv(h, None, wkv_ref, wkr_ref, kvn_ref, wukv_ref, k_ref, v_ref)


def _ctx_kv(x, mod4, row_fn, w, tm):
    nb, ln, d = x.shape
    weights = [w["wkv"], w["wkr"], w["kvn"], w["wukv"]]
    return pl.pallas_call(
        _ctx_kv_kernel,
        out_shape=(jax.ShapeDtypeStruct((nb, N_HEADS, ln, QK_PAD), BF16),
                   jax.ShapeDtypeStruct((nb, N_HEADS, ln, V_PAD), BF16)),
        grid=(nb, ln // tm),
        in_specs=[pl.BlockSpec((None, tm, d), lambda b, i: (b, i, 0)),
                  _mod_spec(row_fn, 1, d), _mod_spec(row_fn, 0, d)] + [_const_spec(a.shape) for a in weights],
        out_specs=(pl.BlockSpec((None, N_HEADS, tm, QK_PAD), lambda b, i: (b, 0, i, 0)),
                   pl.BlockSpec((None, N_HEADS, tm, V_PAD), lambda b, i: (b, 0, i, 0))),
        compiler_params=_params("parallel", "parallel"),
        name="ctx_kv",
    )(x, mod4, mod4, *weights)


def _attn_simple_kernel(*refs, n_kv, sub):
    q_ref, o_ref = refs[0], refs[-1]
    kv = [(refs[1 + 2 * j], refs[2 + 2 * j]) for j in range(n_kv)]
    for r0 in range(0, q_ref.shape[0], sub):
        q = q_ref[r0:r0 + sub, :]
        s = [_dot_nt(q, k_ref[...]) for k_ref, _ in kv]
        m = s[0].max(axis=-1, keepdims=True)
        for sj in s[1:]:
            m = jnp.maximum(m, sj.max(axis=-1, keepdims=True))
        acc = None
        for sj, (_, v_ref) in zip(s, kv):
            p = jnp.exp2((sj - m).astype(BF16))
            pv = _dot(p, v_ref[...])
            acc = pv if acc is None else acc + pv
        o_ref[r0:r0 + sub, :] = (acc[:, 0:V_HEAD] / acc[:, V_HEAD:V_PAD]).astype(BF16)


def _attention_simple(q, kvs, tq):
    nb, nh, lq, _ = q.shape
    in_specs = [pl.BlockSpec((None, None, tq, QK_PAD), lambda b, h, i: (b, h, i, 0))]
    args = [q]
    for k, v in kvs:
        lk = k.shape[2]
        in_specs.append(pl.BlockSpec((None, None, lk, QK_PAD), lambda b, h, i: (b, h, 0, 0)))
        in_specs.append(pl.BlockSpec((None, None, lk, V_PAD), lambda b, h, i: (b, h, 0, 0)))
        args += [k, v]
    return pl.pallas_call(
        functools.partial(_attn_simple_kernel, n_kv=len(kvs), sub=min(tq, ROW_TILE)),
        out_shape=jax.ShapeDtypeStruct((nb, lq, nh * V_HEAD), BF16),
        grid=(nb, nh, lq // tq),
        in_specs=in_specs,
        out_specs=pl.BlockSpec((None, tq, V_HEAD), lambda b, h, i: (b, i, h)),
        compiler_params=_params("parallel", "parallel", "parallel"),
        name="attn_ctx",
    )(*args)


def _attn_pipe_kernel(*refs, n_kv, tq):
    q_ref = refs[0]
    k_refs = refs[1:1 + n_kv]
    v_refs = refs[1 + n_kv:1 + 2 * n_kv]
    o_ref = refs[1 + 2 * n_kv]
    scratch = refs[2 + 2 * n_kv:]
    s_bufs = [scratch[2 * j:2 * j + 2] for j in range(n_kv)]
    p_bufs = [scratch[2 * n_kv + 2 * j:2 * n_kv + 2 * j + 2] for j in range(n_kv)]

    @pl.when(pl.program_id(0) == 0)
    def _():
        for s_buf, p_buf in zip(s_bufs, p_bufs):
            s_buf[1][...] = jnp.zeros(s_buf[1].shape, F32)
            p_buf[0][...] = jnp.ones(p_buf[0].shape, BF16)

    for par in range(2):
        rows = slice(par * tq, (par + 1) * tq)
        m = None
        for s_buf in s_bufs:
            mj = s_buf[1 - par][...].max(axis=-1, keepdims=True)
            m = mj if m is None else jnp.maximum(m, mj)
        for s_buf, p_buf in zip(s_bufs, p_bufs):
            p_buf[1 - par][...] = jnp.exp2((s_buf[1 - par][...] - m).astype(BF16))
        q = q_ref[rows, :]
        for k_ref, s_buf in zip(k_refs, s_bufs):
            s_buf[par][...] = _dot_nt(q, k_ref[...])
        acc = None
        for p_buf, v_ref in zip(p_bufs, v_refs):
            pv = _dot(p_buf[par][...], v_ref[...])
            acc = pv if acc is None else acc + pv
        o_ref[rows, :] = (acc[:, 0:V_HEAD] / acc[:, V_HEAD:V_PAD]).astype(BF16)


def _attention(q, kvs, tq):
    nb, nh, lq, _ = q.shape
    npair = lq // (2 * tq)
    n_steps = nb * nh * npair

    def cur(g):
        t = jnp.minimum(g, n_steps - 1)
        return t // (nh * npair), (t // npair) % nh, t % npair

    def prev(g):
        return cur(jnp.maximum(g - 1, 0))

    def kv_spec(arr, which):
        def index_map(g):
            b, h, _ = which(g)
            return (b, h, 0, 0)
        return pl.BlockSpec((None, None) + arr.shape[2:], index_map)

    def q_map(g):
        b, h, i = cur(g)
        return (b, h, i, 0)

    def o_map(g):
        b, h, i = prev(g)
        return (b, i, h)

    in_specs = [pl.BlockSpec((None, None, 2 * tq, QK_PAD), q_map)]
    in_specs += [kv_spec(k, cur) for k, _ in kvs] + [kv_spec(v, prev) for _, v in kvs]
    return pl.pallas_call(
        functools.partial(_attn_pipe_kernel, n_kv=len(kvs), tq=tq),
        out_shape=jax.ShapeDtypeStruct((nb, lq, nh * V_HEAD), BF16),
        grid=(n_steps + 1,),
        in_specs=in_specs,
        out_specs=pl.BlockSpec((None, 2 * tq, V_HEAD), o_map),
        scratch_shapes=[pltpu.VMEM((tq, k.shape[2]), F32) for k, _ in kvs for _ in range(2)]
        + [pltpu.VMEM((tq, k.shape[2]), BF16) for k, _ in kvs for _ in range(2)],
        compiler_params=_params("arbitrary"),
        name="attn",
    )(q, *[k for k, _ in kvs], *[v for _, v in kvs])


def _top2_of4(a, b, c, d):
    m1, n1 = jnp.maximum(a, b), jnp.minimum(a, b)
    m2, n2 = jnp.maximum(c, d), jnp.minimum(c, d)
    return jnp.maximum(m1, m2) + jnp.maximum(jnp.minimum(m1, m2), jnp.maximum(n1, n2))


def _merge_kernel(x_ref, gm_ref, att_ref, gate_ref, wa_ref, wb_ref, wo_ref, g1_ref, lg_ref, lb_ref,
                  sc2_ref, sh2_ref, wr_ref, br_ref, lat_ref, h2_ref, eid_ref, ewt_ref):
    d = x_ref.shape[1]
    ya = _dot(gm_ref[...], wa_ref[...])
    yb = _dot(att_ref[...], wb_ref[...])
    y = gate_ref[:, 0:d].astype(F32) * ya + gate_ref[:, d:2 * d].astype(F32) * yb
    mix = _dot(y.astype(BF16), wo_ref[...])
    lat = _layer_norm(ALPHA * x_ref[...] + g1_ref[...] * mix, lg_ref[...], lb_ref[...])
    lat_ref[...] = lat
    h2 = lat * (1.0 + sc2_ref[...]) + sh2_ref[...]
    _to_row_tiles(h2_ref, h2)

    h2_hi = h2.astype(BF16)
    h2_lo = (h2 - h2_hi.astype(F32)).astype(BF16)
    l_hi = _dot(h2_hi, wr_ref[...])
    logits = l_hi[:, 0:LANES] + l_hi[:, LANES:2 * LANES] + _dot(h2_lo, wr_ref[:, 0:LANES])
    aff_t = jax.nn.sigmoid(logits.T[0:N_EXPERTS, :])
    sel_t = aff_t + br_ref[...]
    aff = [aff_t[e:e + 1, :] for e in range(N_EXPERTS)]
    sel = [sel_t[e:e + 1, :] for e in range(N_EXPERTS)]
    epg = EXPERTS_PER_GROUP
    best = _top2_of4(*sel[0:epg])
    grp = jnp.zeros_like(best, dtype=jnp.int32)
    for g in range(1, N_EXPERT_GROUPS):
        sc = _top2_of4(*sel[g * epg:(g + 1) * epg])
        better = sc > best
        best = jnp.where(better, sc, best)
        grp = jnp.where(better, g, grp)

    def pick(vals, j):
        out = vals[j]
        for g in range(1, N_EXPERT_GROUPS):
            out = jnp.where(grp == g, vals[g * epg + j], out)
        return out

    s_in = [pick(sel, j) for j in range(epg)]
    a_in = [pick(aff, j) for j in range(epg)]
    i1 = jnp.zeros_like(grp)
    v1, w1 = s_in[0], a_in[0]
    for j in range(1, epg):
        better = s_in[j] > v1
        v1 = jnp.where(better, s_in[j], v1)
        w1 = jnp.where(better, a_in[j], w1)
        i1 = jnp.where(better, j, i1)
    i2 = jnp.full_like(grp, -1)
    v2 = jnp.full_like(v1, -jnp.inf)
    w2 = jnp.zeros_like(w1)
    for j in range(epg):
        better = (i1 != j) & (s_in[j] > v2)
        v2 = jnp.where(better, s_in[j], v2)
        w2 = jnp.where(better, a_in[j], w2)
        i2 = jnp.where(better, j, i2)
    tot = w1 + w2
    eid_ref[0:1, :] = grp * epg + i1
    eid_ref[1:2, :] = grp * epg + i2
    ewt_ref[0:1, :] = w1 / tot
    ewt_ref[1:2, :] = w2 / tot


def _merge(x, gm, att, gate, mod4, row_fn, w, wr_pad, br_col, tm):
    nb, ln, d = x.shape
    nt = ln // tm
    weights = [w["wa"], w["wb"], w["wo"]]

    def tile3(width):
        return pl.BlockSpec((None, tm, width), lambda b, i: (b, i, 0))

    in_specs = [tile3(d), tile3(GM_WIDTH), tile3(N_HEADS * V_HEAD), tile3(2 * d)] + [
        _const_spec(a.shape) for a in weights] + [
        _mod_spec(row_fn, 2, d), _const_spec(w["ln1g"].shape), _const_spec(w["ln1b"].shape),
        _mod_spec(row_fn, 4, d), _mod_spec(row_fn, 3, d), _const_spec(wr_pad.shape), _const_spec(br_col.shape)]
    out_shape = (
        jax.ShapeDtypeStruct((nb, ln, d), F32),
        jax.ShapeDtypeStruct((nb, ln * SUBLANES, d // SUBLANES), F32),
        jax.ShapeDtypeStruct((nb * nt, TOP_K, tm), jnp.int32),
        jax.ShapeDtypeStruct((nb * nt, TOP_K, tm), F32),
    )
    out_specs = (
        tile3(d), pl.BlockSpec((None, tm * SUBLANES, d // SUBLANES), lambda b, i: (b, i, 0)),
        pl.BlockSpec((None, TOP_K, tm), lambda b, i: (b * nt + i, 0, 0)),
        pl.BlockSpec((None, TOP_K, tm), lambda b, i: (b * nt + i, 0, 0)),
    )
    return pl.pallas_call(
        _merge_kernel,
        out_shape=out_shape,
        grid=(nb, nt),
        in_specs=in_specs,
        out_specs=out_specs,
        compiler_params=_params("parallel", "parallel"),
        name="merge",
    )(x, gm, att, gate, *weights, mod4, w["ln1g"], w["ln1b"], mod4, mod4, wr_pad, br_col)


def _load_indices(pos_ref, t, idx_smem, sem):
    cp = pltpu.make_async_copy(pos_ref.at[t], idx_smem, sem)
    cp.start()
    cp.wait()


def _dispatch_kernel(meta_ref, pos_ref, h_ref, buf_hbm, idx_smem, zbuf, sem_i, sem_r, sem_z):
    tm = h_ref.shape[0] // SUBLANES
    tmb = zbuf.shape[0]
    n_tiles = buf_hbm.shape[0] // tmb

    @pl.when(pl.program_id(0) == 0)
    def _():
        zbuf[...] = jnp.zeros_like(zbuf)
        n_valid = meta_ref[2 * N_EXPERTS]
        fills = [(meta_ref[N_EXPERTS + e] > 0, meta_ref[e]) for e in range(N_EXPERTS)]
        fills += [(n_valid + j < n_tiles, n_valid + j) for j in range(N_EXPERTS + 1)]
        copies = [(need, pltpu.make_async_copy(zbuf, buf_hbm.at[pl.ds(pl.multiple_of(tile * tmb, tmb), tmb)], sem_z))
                  for need, tile in fills]
        for need, cp in copies:
            pl.when(need)(cp.start)
        for need, cp in copies:
            pl.when(need)(cp.wait)

    _load_indices(pos_ref, pl.program_id(0), idx_smem, sem_i)

    def body(j, carry):
        for k in range(TOP_K):
            dst = idx_smem[k * tm + j]
            pltpu.make_async_copy(_row_tile(h_ref, j), _row_tile(buf_hbm, dst), sem_r).start()
        return carry

    lax.fori_loop(0, tm, body, 0, unroll=DMA_UNROLL)
    for k in range(TOP_K):
        pltpu.make_async_copy(h_ref, buf_hbm.at[pl.ds(0, tm * SUBLANES)], sem_r).wait()


def _dispatch(meta, pos, h2, n_rows, tm):
    n_tok = h2.shape[0] // SUBLANES
    grid_spec = pltpu.PrefetchScalarGridSpec(
        num_scalar_prefetch=1,
        grid=(n_tok // tm,),
        in_specs=[_const_spec(pos.shape), pl.BlockSpec((tm * SUBLANES, LANES), lambda t, meta: (t, 0))],
        out_specs=pl.BlockSpec(memory_space=pl.ANY),
        scratch_shapes=[pltpu.SMEM((TOP_K * tm,), jnp.int32), pltpu.VMEM((MOE_TILE * SUBLANES, LANES), F32),
                        pltpu.SemaphoreType.DMA, pltpu.SemaphoreType.DMA, pltpu.SemaphoreType.DMA],
    )
    return pl.pallas_call(
        _dispatch_kernel,
        out_shape=jax.ShapeDtypeStruct((n_rows * SUBLANES, LANES), F32),
        grid_spec=grid_spec,
        compiler_params=_params("arbitrary"),
        name="dispatch",
    )(meta, pos, h2)


def _ffn_kernel(te_ref, nr_ref, nv_ref, x_ref, wg_ref, wu_ref, wd_ref, o_ref):
    nr = nr_ref[pl.program_id(0)]

    @pl.when(nr > 0)
    def _():
        x = _from_row_tiles(x_ref, x_ref.shape[0] // SUBLANES)
        row = lax.broadcasted_iota(jnp.int32, x.shape, 0)
        x = jnp.where(row < nr, x, 0.0).astype(BF16)
        g = _dot(x, wg_ref[...])
        u = _dot(x, wu_ref[...])
        hm = (g * jax.nn.sigmoid(g) * u).astype(BF16)
        _to_row_tiles(o_ref, _dot(hm, wd_ref[...]))

    @pl.when(nr == 0)
    def _():
        o_ref[...] = jnp.zeros_like(o_ref)


def _ffn(tile_e, tile_rows, n_valid, buf, wg, wu, wd):
    n_tiles = tile_e.shape[0]
    tmb = buf.shape[0] // n_tiles
    d, de = wg.shape[1], wg.shape[2]

    def xmap(t, te, nr, nv):
        return (jnp.minimum(t, nv[0] - 1), 0)

    grid_spec = pltpu.PrefetchScalarGridSpec(
        num_scalar_prefetch=3,
        grid=(n_tiles,),
        in_specs=[
            pl.BlockSpec((tmb, LANES), xmap),
            pl.BlockSpec((None, d, de), lambda t, te, nr, nv: (te[t], 0, 0)),
            pl.BlockSpec((None, d, de), lambda t, te, nr, nv: (te[t], 0, 0)),
            pl.BlockSpec((None, de, d), lambda t, te, nr, nv: (te[t], 0, 0)),
        ],
        out_specs=pl.BlockSpec((tmb, LANES), lambda t, te, nr, nv: (t, 0)),
    )
    return pl.pallas_call(
        _ffn_kernel,
        out_shape=jax.ShapeDtypeStruct((n_tiles * tmb, LANES), F32),
        grid_spec=grid_spec,
        compiler_params=_params("arbitrary"),
        name="ffn",
    )(tile_e, tile_rows, n_valid, buf, wg, wu, wd)


def _combine_kernel(pos_ref, e_hbm, x_ref, wt_ref, g2_ref, lg_ref, lb_ref, o_ref, idx_smem, fbuf, sem_i, sem_r):
    tm = x_ref.shape[0]
    rows = fbuf.shape[0] // SUBLANES
    _load_indices(pos_ref, pl.program_id(0) * pl.num_programs(1) + pl.program_id(1), idx_smem, sem_i)

    def body(j, carry):
        src = idx_smem[j]
        pltpu.make_async_copy(_row_tile(e_hbm, src), _row_tile(fbuf, j), sem_r).start()
        return carry

    lax.fori_loop(0, rows, body, 0, unroll=DMA_UNROLL)
    pltpu.make_async_copy(e_hbm.at[pl.ds(0, rows * SUBLANES)], fbuf, sem_r).wait()
    f = wt_ref[:, 0:1] * _from_row_tiles(fbuf, tm) + wt_ref[:, 1:2] * _from_row_tiles(fbuf, tm, row0=tm)
    o_ref[...] = _layer_norm(ALPHA * x_ref[...] + g2_ref[...] * f, lg_ref[...], lb_ref[...])


def _combine(pos, eout, wt, lat, mod4, row_fn, ln_g, ln_b, tm):
    nb, ln, d = lat.shape
    ntb = ln // tm
    return pl.pallas_call(
        _combine_kernel,
        out_shape=jax.ShapeDtypeStruct((nb, ln, d), F32),
        grid=(nb, ntb),
        in_specs=[
            _const_spec(pos.shape),
            pl.BlockSpec(memory_space=pl.ANY),
            pl.BlockSpec((None, tm, d), lambda b, i: (b, i, 0)),
            pl.BlockSpec((None, tm, TOP_K), lambda b, i: (b * ntb + i, 0, 0)),
            _mod_spec(row_fn, 5, d), _const_spec(ln_g.shape), _const_spec(ln_b.shape),
        ],
        out_specs=pl.BlockSpec((None, tm, d), lambda b, i: (b, i, 0)),
        scratch_shapes=[
            pltpu.SMEM((TOP_K * tm,), jnp.int32),
            pltpu.VMEM((TOP_K * tm * SUBLANES, LANES), F32),
            pltpu.SemaphoreType.DMA,
            pltpu.SemaphoreType.DMA,
        ],
        compiler_params=_params("arbitrary", "arbitrary"),
        name="combine",
    )(pos, eout, lat, wt, mod4, ln_g, ln_b)


def _route(eid):
    n_tok_tiles, _, tm = eid.shape
    n_assign = n_tok_tiles * tm * TOP_K
    e_flat = eid.reshape(-1)
    experts = jnp.arange(N_EXPERTS, dtype=jnp.int32)
    onehot = (e_flat[:, None] == experts[None, :]).astype(jnp.int32)
    csum = jnp.cumsum(onehot, axis=0)
    rank = jnp.sum(onehot * csum, axis=1) - 1
    counts = csum[-1]
    tiles_per = (counts + MOE_TILE - 1) // MOE_TILE
    tile_end = jnp.cumsum(tiles_per)
    tile_start = tile_end - tiles_per
    pos = (jnp.sum(onehot * tile_start[None, :], axis=1) * MOE_TILE + rank).astype(jnp.int32)
    n_tiles = -(-(n_assign + N_EXPERTS * (MOE_TILE - 1)) // MOE_TILE)
    t_idx = jnp.arange(n_tiles, dtype=jnp.int32)
    tile_e = jnp.minimum(jnp.sum((t_idx[:, None] >= tile_end[None, :]).astype(jnp.int32), axis=1), N_EXPERTS - 1)
    own = (tile_e[:, None] == experts[None, :]).astype(jnp.int32)
    left = jnp.sum(own * (counts - (t_idx[:, None] - tile_start[None, :]) * MOE_TILE), axis=1)
    tile_rows = jnp.where(t_idx < tile_end[-1], jnp.clip(left, 0, MOE_TILE), 0).astype(jnp.int32)
    n_valid = tile_end[-1:].astype(jnp.int32)
    meta = jnp.concatenate([tile_end - 1, tiles_per, n_valid]).astype(jnp.int32)
    return tile_e.astype(jnp.int32), tile_rows, n_valid, meta, pos.reshape(n_tok_tiles, TOP_K * tm), n_tiles


def _moe(lat, h2, eid, ewt, mod4, row_fn, wg, wu, wd, ln_g, ln_b, tm):
    d = lat.shape[-1]
    tile_e, tile_rows, n_valid, meta, pos, n_tiles = _route(eid)
    buf = _dispatch(meta, pos, h2.reshape(-1, LANES), n_tiles * MOE_TILE, tm)
    eout = _ffn(tile_e, tile_rows, n_valid, buf, wg, wu, wd)
    return _combine(pos, eout, ewt.transpose(0, 2, 1), lat, mod4, row_fn, ln_g, ln_b, tm)


def _swap_halves(a):
    h = a.shape[-1] // 2
    return jnp.concatenate([a[..., h:], a[..., :h]], axis=-1)


def _layer_weights(l, w_in, gm_ln_g, gm_ln_b, gm_ws, gm_bs, q_norm, w_uq, kv_norm, w_ukv, w_proj_a, w_proj_b,
                   b_merge, w_out, ln1_g, ln1_b):
    wi = w_in[l]
    kr = wi[:, OFF_KR:OFF_G]
    uq = w_uq[l].reshape(Q_LORA, N_HEADS, QK_NOPE + QK_ROPE)
    uq_r = uq[..., QK_NOPE:]
    uq_ext = jnp.concatenate([uq[..., :QK_NOPE], uq_r, _swap_halves(uq_r)], axis=-1)
    return dict(
        wu=wi[:, 0:GM_WIDTH].astype(BF16),
        wv=wi[:, GM_WIDTH:2 * GM_WIDTH].astype(BF16),
        wq=wi[:, OFF_Q:OFF_KV].astype(BF16),
        wkv=wi[:, OFF_KV:OFF_KR].astype(BF16),
        wkr=jnp.concatenate([kr, _swap_halves(kr)], axis=-1).astype(BF16),
        wg=wi[:, OFF_G:].astype(BF16),
        lng=gm_ln_g[l].reshape(1, -1), lnb=gm_ln_b[l].reshape(1, -1),
        ws=gm_ws[l].astype(BF16), bs=gm_bs[l].reshape(GM_GROUPS, CHUNK, 1),
        qn=q_norm[l].reshape(1, -1), wuq=uq_ext.reshape(Q_LORA, N_HEADS * QK_PAD).astype(BF16),
        kvn=kv_norm[l].reshape(1, -1), wukv=w_ukv[l].astype(BF16),
        bm=b_merge[l].reshape(1, -1),
        wa=w_proj_a[l].astype(BF16), wb=w_proj_b[l].astype(BF16), wo=w_out[l].astype(BF16),
        ln1g=ln1_g[l].reshape(1, -1), ln1b=ln1_b[l].reshape(1, -1),
    )


def _rope_table(seq):
    rows = seq // GRID_W
    r = jnp.repeat(jnp.arange(rows), GRID_W).astype(F32)
    c = jnp.tile(jnp.arange(GRID_W), rows).astype(F32)
    n_freq = QK_ROPE // 4
    inv = ROPE_THETA ** (-jnp.arange(n_freq, dtype=F32) / n_freq)
    ang = jnp.concatenate([r[:, None] * inv, c[:, None] * inv], axis=-1)
    cos, sin = jnp.cos(ang), jnp.sin(ang)
    return jnp.concatenate([cos, cos, -sin, sin], axis=-1)


def kernel(x, c, ctx, c_ctx, w_ada, b_ada, w_in, gm_ln_g, gm_ln_b, gm_ws, gm_bs, q_norm, w_uq, kv_norm, w_ukv,
           w_proj_a, w_proj_b, b_merge, w_out, ln1_g, ln1_b, w_router, b_router, w_e_gate, w_e_up, w_e_down,
           ln2_g, ln2_b):
    nb, seq, d = x.shape
    depth = w_ada.shape[0]
    tm = ROW_TILE
    tq = 2 * ROW_TILE if seq % (2 * ROW_TILE) == 0 else ROW_TILE

    mod_rows = -(-(nb + 1) // 8) * 8
    cc = jnp.concatenate([c, c_ctx[None, :], jnp.zeros((mod_rows - nb - 1, d), F32)], axis=0)
    mod = _ada(cc, w_ada, b_ada)
    tab = _rope_table(seq)
    wr_hi = w_router.astype(BF16)
    wr_lo = (w_router - wr_hi.astype(F32)).astype(BF16)
    lane_pad = jnp.zeros((d, LANES - N_EXPERTS), BF16)
    wr_pad = jnp.concatenate([wr_hi, lane_pad, wr_lo, lane_pad], axis=1)
    br_col = b_router.reshape(N_EXPERTS, 1)

    def lat_row(b):
        return b

    def ctx_row(b):
        return nb

    lat, cx = x, ctx
    for l in range(depth):
        last = l == depth - 1
        w = _layer_weights(l, w_in, gm_ln_g, gm_ln_b, gm_ws, gm_bs, q_norm, w_uq, kv_norm, w_ukv,
                           w_proj_a, w_proj_b, b_merge, w_out, ln1_g, ln1_b)
        wg, wu, wd = w_e_gate[l].astype(BF16), w_e_up[l].astype(BF16), w_e_down[l].astype(BF16)
        ln2 = (ln2_g[l].reshape(1, -1), ln2_b[l].reshape(1, -1))
        mod4 = mod[l].reshape(mod_rows, 6, 1, d)

        if last:
            ck, cv = _ctx_kv(cx, mod4, ctx_row, w, tm)
        else:
            cgm, cq, ck, cv, cgate = _inproj(cx, mod4, ctx_row, w, None, tm)
        gm, q, k, v, gate = _inproj(lat, mod4, lat_row, w, tab, tq)
        att = (_attention if seq % (2 * tq) == 0 else _attention_simple)(q, [(ck, cv), (k, v)], tq)
        lat1, h2, eid, ewt = _merge(lat, gm, att, gate, mod4, lat_row, w, wr_pad, br_col, tq)
        lat = _moe(lat1, h2, eid, ewt, mod4, lat_row, wg, wu, wd, *ln2, tq)
        if not last:
            catt = _attention_simple(cq, [(ck, cv)], tm)
            cx1, ch2, ceid, cewt = _merge(cx, cgm, catt, cgate, mod4, ctx_row, w, wr_pad, br_col, tm)
            cx = _moe(cx1, ch2, ceid, cewt, mod4, ctx_row, wg, wu, wd, *ln2, tm)
    return lat
```

```python
import functools
import math

import jax
import jax.numpy as jnp
from jax import lax
from jax.experimental import pallas as pl
from jax.experimental.pallas import tpu as pltpu

F32 = jnp.float32
BF16 = jnp.bfloat16

D_MODEL = 1024
DEPTH = 2
GRID_W = 64
CHUNK = 128
GM_GROUPS = 8
GM_WIDTH = 1024
N_HEADS = 8
Q_LORA = 384
KV_LORA = 256
QK_NOPE = 128
QK_ROPE = 64
V_HEAD = 128
ROPE_THETA = 10000.0
N_EXPERTS = 16
N_EXPERT_GROUPS = 4
EXPERTS_PER_GROUP = N_EXPERTS // N_EXPERT_GROUPS
TOP_K = 2
D_EXPERT = 512
LN_EPS = 1e-5
ALPHA = (2.0 * DEPTH) ** 0.25

OFF_Q = 2 * GM_WIDTH
OFF_KV = OFF_Q + Q_LORA
OFF_KR = OFF_KV + KV_LORA
OFF_G = OFF_KR + QK_ROPE

LANES = 128
QK_PAD = 2 * LANES
V_PAD = 2 * LANES
ROW_TILE = 256
MOE_TILE = 512
DMA_UNROLL = 8
SUBLANES = 8
VMEM_LIMIT = 56 * 1024 * 1024

HIGHEST = lax.Precision.HIGHEST
LOG2E = 1.4426950408889634


def _dot(a, b):
    return jnp.dot(a, b, preferred_element_type=F32)


def _dot_nt(a, b):
    return lax.dot_general(a, b, (((1,), (1,)), ((), ())), preferred_element_type=F32)


def _layer_norm(z, g, b):
    mu = jnp.mean(z, axis=-1, keepdims=True)
    zc = z - mu
    var = jnp.mean(zc * zc, axis=-1, keepdims=True)
    return zc * lax.rsqrt(var + LN_EPS) * g + b


def _rms_norm(z, g):
    return z * lax.rsqrt(jnp.mean(z * z, axis=-1, keepdims=True) + LN_EPS) * g


def _to_row_tiles(ref, val, row0=0):
    rows, d = val.shape
    for c in range(d // LANES):
        ref[pl.ds(row0 * SUBLANES + c, rows, stride=SUBLANES), :] = val[:, c * LANES:(c + 1) * LANES]


def _from_row_tiles(ref, rows, row0=0):
    return jnp.concatenate(
        [ref[pl.ds(row0 * SUBLANES + c, rows, stride=SUBLANES), :] for c in range(SUBLANES)], axis=-1)


def _row_tile(ref, r):
    return ref.at[pl.ds(pl.multiple_of(r * SUBLANES, SUBLANES), SUBLANES)]


def _params(*sem):
    return pltpu.CompilerParams(dimension_semantics=sem, vmem_limit_bytes=VMEM_LIMIT)


def _const_spec(shape):
    nd = len(shape)
    return pl.BlockSpec(shape, lambda *_: (0,) * nd, pipeline_mode=pl.Buffered(1))


def _mod_spec(row_fn, k, d):
    return pl.BlockSpec((None, None, 1, d), lambda b, i: (row_fn(b), k, 0, 0))


def _ada_kernel(c_ref, w_ref, b_ref, o_ref):
    cc = c_ref[...]
    s = cc * jax.nn.sigmoid(cc)
    o_ref[...] = jnp.dot(s, w_ref[...], precision=HIGHEST, preferred_element_type=F32) + b_ref[...]


def _ada(cc, w_ada, b_ada):
    depth, d, n = w_ada.shape
    rows = cc.shape[0]
    tn = 1536
    return pl.pallas_call(
        _ada_kernel,
        out_shape=jax.ShapeDtypeStruct((depth, rows, n), F32),
        grid=(depth, n // tn),
        in_specs=[
            pl.BlockSpec((rows, d), lambda l, j: (0, 0)),
            pl.BlockSpec((None, d, tn), lambda l, j: (l, 0, j)),
            pl.BlockSpec((None, 1, tn), lambda l, j: (l, 0, j)),
        ],
        out_specs=pl.BlockSpec((None, rows, tn), lambda l, j: (l, 0, j)),
        compiler_params=_params("parallel", "parallel"),
        name="ada",
    )(cc, w_ada, b_ada.reshape(depth, 1, n))


def _rope(r, tab):
    m = r * tab
    return m + pltpu.roll(m, LANES // 2, axis=1)


def _write_kv(h, tab, wkv_ref, wkr_ref, kvn_ref, wukv_ref, k_ref, v_ref):
    tm = h.shape[0]
    kvn = _rms_norm(_dot(h, wkv_ref[...]), kvn_ref[...]).astype(BF16)
    pkv = _dot(kvn, wukv_ref[...])
    kr = _dot(h, wkr_ref[...])
    if tab is not None:
        kr = _rope(kr, tab)
    lane = lax.broadcasted_iota(jnp.int32, (tm, LANES), 1)
    kr = jnp.where(lane < LANES // 2, kr, 0.0).astype(BF16)
    ones = jnp.ones((tm, V_PAD - V_HEAD), BF16)
    for hh in range(N_HEADS):
        o = hh * (QK_NOPE + V_HEAD)
        k_ref[hh, :, 0:QK_NOPE] = pkv[:, o:o + QK_NOPE].astype(BF16)
        k_ref[hh, :, QK_NOPE:QK_PAD] = kr
        v_ref[hh, :, 0:V_HEAD] = pkv[:, o + QK_NOPE:o + QK_NOPE + V_HEAD].astype(BF16)
        v_ref[hh, :, V_HEAD:V_PAD] = ones


def _inproj_kernel(*refs, use_rope):
    if use_rope:
        (x_ref, sc_ref, sh_ref, wu_ref, wv_ref, wq_ref, wkv_ref, wkr_ref, wg_ref, lng_ref, lnb_ref, ws_ref,
         bs_ref, qn_ref, wuq_ref, kvn_ref, wukv_ref, bm_ref, tab_ref, gm_ref, q_ref, k_ref, v_ref, gate_ref) = refs
        tab = tab_ref[...]
    else:
        (x_ref, sc_ref, sh_ref, wu_ref, wv_ref, wq_ref, wkv_ref, wkr_ref, wg_ref, lng_ref, lnb_ref, ws_ref,
         bs_ref, qn_ref, wuq_ref, kvn_ref, wukv_ref, bm_ref, gm_ref, q_ref, k_ref, v_ref, gate_ref) = refs
        tab = None
    tm = x_ref.shape[0]
    h = (x_ref[...] * (1.0 + sc_ref[...]) + sh_ref[...]).astype(BF16)

    u = jax.nn.gelu(_dot(h, wu_ref[...]))
    v = jax.nn.gelu(_dot(h, wv_ref[...]))
    vn = _layer_norm(v, lng_ref[...], lnb_ref[...]).astype(BF16)
    gd = GM_WIDTH // GM_GROUPS
    for c in range(tm // CHUNK):
        rs = slice(c * CHUNK, (c + 1) * CHUNK)
        for g in range(GM_GROUPS):
            cs = slice(g * gd, (g + 1) * gd)
            s = _dot(ws_ref[g], vn[rs, cs]) + bs_ref[g]
            gm_ref[rs, cs] = (u[rs, cs] * s).astype(BF16)

    scale = (QK_NOPE + QK_ROPE) ** -0.5 * LOG2E
    qn = _rms_norm(_dot(h, wq_ref[...]), qn_ref[...] * scale).astype(BF16)
    pq = _dot(qn, wuq_ref[...])
    for hh in range(N_HEADS):
        o = hh * QK_PAD
        q_ref[hh, :, 0:QK_NOPE] = pq[:, o:o + QK_NOPE].astype(BF16)
        qr = pq[:, o + QK_NOPE:o + QK_PAD]
        q_ref[hh, :, QK_NOPE:QK_PAD] = (qr if tab is None else _rope(qr, tab)).astype(BF16)

    _write_kv(h, tab, wkv_ref, wkr_ref, kvn_ref, wukv_ref, k_ref, v_ref)
    gate_ref[...] = jax.nn.sigmoid(_dot(h, wg_ref[...]) + bm_ref[...]).astype(BF16)


def _inproj(x, mod4, row_fn, w, tab, tm):
    nb, ln, d = x.shape
    weights = [w["wu"], w["wv"], w["wq"], w["wkv"], w["wkr"], w["wg"], w["lng"], w["lnb"], w["ws"],
               w["bs"], w["qn"], w["wuq"], w["kvn"], w["wukv"], w["bm"]]
    in_specs = [
        pl.BlockSpec((None, tm, d), lambda b, i: (b, i, 0)),
        _mod_spec(row_fn, 1, d), _mod_spec(row_fn, 0, d),
    ] + [_const_spec(a.shape) for a in weights]
    args = [x, mod4, mod4] + weights
    if tab is not None:
        in_specs.append(pl.BlockSpec((tm, LANES), lambda b, i: (i, 0)))
        args.append(tab)
    out_shape = (
        jax.ShapeDtypeStruct((nb, ln, GM_WIDTH), BF16),
        jax.ShapeDtypeStruct((nb, N_HEADS, ln, QK_PAD), BF16),
        jax.ShapeDtypeStruct((nb, N_HEADS, ln, QK_PAD), BF16),
        jax.ShapeDtypeStruct((nb, N_HEADS, ln, V_PAD), BF16),
        jax.ShapeDtypeStruct((nb, ln, 2 * d), BF16),
    )
    out_specs = (
        pl.BlockSpec((None, tm, GM_WIDTH), lambda b, i: (b, i, 0)),
        pl.BlockSpec((None, N_HEADS, tm, QK_PAD), lambda b, i: (b, 0, i, 0)),
        pl.BlockSpec((None, N_HEADS, tm, QK_PAD), lambda b, i: (b, 0, i, 0)),
        pl.BlockSpec((None, N_HEADS, tm, V_PAD), lambda b, i: (b, 0, i, 0)),
        pl.BlockSpec((None, tm, 2 * d), lambda b, i: (b, i, 0)),
    )
    return pl.pallas_call(
        functools.partial(_inproj_kernel, use_rope=tab is not None),
        out_shape=out_shape,
        grid=(nb, ln // tm),
        in_specs=in_specs,
        out_specs=out_specs,
        compiler_params=_params("parallel", "parallel"),
        name="inproj",
    )(*args)


def _ctx_kv_kernel(x_ref, sc_ref, sh_ref, wkv_ref, wkr_ref, kvn_ref, wukv_ref, k_ref, v_ref):
    h = (x_ref[...] * (1.0 + sc_ref[...]) + sh_ref[...]).astype(BF16)
    _write_kv(h, None, wkv_ref, wkr_ref, kvn_ref, wukv_ref, k_ref, v_ref)


def _ctx_kv(x, mod4, row_fn, w, tm):
    nb, ln, d = x.shape
    weights = [w["wkv"], w["wkr"], w["kvn"], w["wukv"]]
    return pl.pallas_call(
        _ctx_kv_kernel,
        out_shape=(jax.ShapeDtypeStruct((nb, N_HEADS, ln, QK_PAD), BF16),
                   jax.ShapeDtypeStruct((nb, N_HEADS, ln, V_PAD), BF16)),
        grid=(nb, ln // tm),
        in_specs=[pl.BlockSpec((None, tm, d), lambda b, i: (b, i, 0)),
                  _mod_spec(row_fn, 1, d), _mod_spec(row_fn, 0, d)] + [_const_spec(a.shape) for a in weights],
        out_specs=(pl.BlockSpec((None, N_HEADS, tm, QK_PAD), lambda b, i: (b, 0, i, 0)),
                   pl.BlockSpec((None, N_HEADS, tm, V_PAD), lambda b, i: (b, 0, i, 0))),
        compiler_params=_params("parallel", "parallel"),
        name="ctx_kv",
    )(x, mod4, mod4, *weights)


def _attn_simple_kernel(*refs, n_kv, sub):
    q_ref, o_ref = refs[0], refs[-1]
    kv = [(refs[1 + 2 * j], refs[2 + 2 * j]) for j in range(n_kv)]
    for r0 in range(0, q_ref.shape[0], sub):
        q = q_ref[r0:r0 + sub, :]
        s = [_dot_nt(q, k_ref[...]) for k_ref, _ in kv]
        m = s[0].max(axis=-1, keepdims=True)
        for sj in s[1:]:
            m = jnp.maximum(m, sj.max(axis=-1, keepdims=True))
        acc = None
        for sj, (_, v_ref) in zip(s, kv):
            p = jnp.exp2((sj - m).astype(BF16))
            pv = _dot(p, v_ref[...])
            acc = pv if acc is None else acc + pv
        o_ref[r0:r0 + sub, :] = (acc[:, 0:V_HEAD] / acc[:, V_HEAD:V_PAD]).astype(BF16)


def _attention_simple(q, kvs, tq):
    nb, nh, lq, _ = q.shape
    in_specs = [pl.BlockSpec((None, None, tq, QK_PAD), lambda b, h, i: (b, h, i, 0))]
    args = [q]
    for k, v in kvs:
        lk = k.shape[2]
        in_specs.append(pl.BlockSpec((None, None, lk, QK_PAD), lambda b, h, i: (b, h, 0, 0)))
        in_specs.append(pl.BlockSpec((None, None, lk, V_PAD), lambda b, h, i: (b, h, 0, 0)))
        args += [k, v]
    return pl.pallas_call(
        functools.partial(_attn_simple_kernel, n_kv=len(kvs), sub=min(tq, ROW_TILE)),
        out_shape=jax.ShapeDtypeStruct((nb, lq, nh * V_HEAD), BF16),
        grid=(nb, nh, lq // tq),
        in_specs=in_specs,
        out_specs=pl.BlockSpec((None, tq, V_HEAD), lambda b, h, i: (b, i, h)),
        compiler_params=_params("parallel", "parallel", "parallel"),
        name="attn_ctx",
    )(*args)


def _attn_pipe_kernel(*refs, n_kv, tq):
    q_ref = refs[0]
    k_refs = refs[1:1 + n_kv]
    v_refs = refs[1 + n_kv:1 + 2 * n_kv]
    o_ref = refs[1 + 2 * n_kv]
    scratch = refs[2 + 2 * n_kv:]
    s_bufs = [scratch[2 * j:2 * j + 2] for j in range(n_kv)]
    p_bufs = [scratch[2 * n_kv + 2 * j:2 * n_kv + 2 * j + 2] for j in range(n_kv)]

    @pl.when(pl.program_id(0) == 0)
    def _():
        for s_buf, p_buf in zip(s_bufs, p_bufs):
            s_buf[1][...] = jnp.zeros(s_buf[1].shape, F32)
            p_buf[0][...] = jnp.ones(p_buf[0].shape, BF16)

    for par in range(2):
        rows = slice(par * tq, (par + 1) * tq)
        m = None
        for s_buf in s_bufs:
            mj = s_buf[1 - par][...].max(axis=-1, keepdims=True)
            m = mj if m is None else jnp.maximum(m, mj)
        for s_buf, p_buf in zip(s_bufs, p_bufs):
            p_buf[1 - par][...] = jnp.exp2((s_buf[1 - par][...] - m).astype(BF16))
        q = q_ref[rows, :]
        for k_ref, s_buf in zip(k_refs, s_bufs):
            s_buf[par][...] = _dot_nt(q, k_ref[...])
        acc = None
        for p_buf, v_ref in zip(p_bufs, v_refs):
            pv = _dot(p_buf[par][...], v_ref[...])
            acc = pv if acc is None else acc + pv
        o_ref[rows, :] = (acc[:, 0:V_HEAD] / acc[:, V_HEAD:V_PAD]).astype(BF16)


def _attention(q, kvs, tq):
    nb, nh, lq, _ = q.shape
    npair = lq // (2 * tq)
    n_steps = nb * nh * npair

    def cur(g):
        t = jnp.minimum(g, n_steps - 1)
        return t // (nh * npair), (t // npair) % nh, t % npair

    def prev(g):
        return cur(jnp.maximum(g - 1, 0))

    def kv_spec(arr, which):
        def index_map(g):
            b, h, _ = which(g)
            return (b, h, 0, 0)
        return pl.BlockSpec((None, None) + arr.shape[2:], index_map)

    def q_map(g):
        b, h, i = cur(g)
        return (b, h, i, 0)

    def o_map(g):
        b, h, i = prev(g)
        return (b, i, h)

    in_specs = [pl.BlockSpec((None, None, 2 * tq, QK_PAD), q_map)]
    in_specs += [kv_spec(k, cur) for k, _ in kvs] + [kv_spec(v, prev) for _, v in kvs]
    return pl.pallas_call(
        functools.partial(_attn_pipe_kernel, n_kv=len(kvs), tq=tq),
        out_shape=jax.ShapeDtypeStruct((nb, lq, nh * V_HEAD), BF16),
        grid=(n_steps + 1,),
        in_specs=in_specs,
        out_specs=pl.BlockSpec((None, 2 * tq, V_HEAD), o_map),
        scratch_shapes=[pltpu.VMEM((tq, k.shape[2]), F32) for k, _ in kvs for _ in range(2)]
        + [pltpu.VMEM((tq, k.shape[2]), BF16) for k, _ in kvs for _ in range(2)],
        compiler_params=_params("arbitrary"),
        name="attn",
    )(q, *[k for k, _ in kvs], *[v for _, v in kvs])


def _top2_of4(a, b, c, d):
    m1, n1 = jnp.maximum(a, b), jnp.minimum(a, b)
    m2, n2 = jnp.maximum(c, d), jnp.minimum(c, d)
    return jnp.maximum(m1, m2) + jnp.maximum(jnp.minimum(m1, m2), jnp.maximum(n1, n2))


def _merge_kernel(x_ref, gm_ref, att_ref, gate_ref, wa_ref, wb_ref, wo_ref, g1_ref, lg_ref, lb_ref,
                  sc2_ref, sh2_ref, wr_ref, br_ref, lat_ref, h2_ref, eid_ref, ewt_ref):
    d = x_ref.shape[1]
    ya = _dot(gm_ref[...], wa_ref[...])
    yb = _dot(att_ref[...], wb_ref[...])
    y = gate_ref[:, 0:d].astype(F32) * ya + gate_ref[:, d:2 * d].astype(F32) * yb
    mix = _dot(y.astype(BF16), wo_ref[...])
    lat = _layer_norm(ALPHA * x_ref[...] + g1_ref[...] * mix, lg_ref[...], lb_ref[...])
    lat_ref[...] = lat
    h2 = lat * (1.0 + sc2_ref[...]) + sh2_ref[...]
    _to_row_tiles(h2_ref, h2)

    h2_hi = h2.astype(BF16)
    h2_lo = (h2 - h2_hi.astype(F32)).astype(BF16)
    l_hi = _dot(h2_hi, wr_ref[...])
    logits = l_hi[:, 0:LANES] + l_hi[:, LANES:2 * LANES] + _dot(h2_lo, wr_ref[:, 0:LANES])
    aff_t = jax.nn.sigmoid(logits.T[0:N_EXPERTS, :])
    sel_t = aff_t + br_ref[...]
    aff = [aff_t[e:e + 1, :] for e in range(N_EXPERTS)]
    sel = [sel_t[e:e + 1, :] for e in range(N_EXPERTS)]
    epg = EXPERTS_PER_GROUP
    best = _top2_of4(*sel[0:epg])
    grp = jnp.zeros_like(best, dtype=jnp.int32)
    for g in range(1, N_EXPERT_GROUPS):
        sc = _top2_of4(*sel[g * epg:(g + 1) * epg])
        better = sc > best
        best = jnp.where(better, sc, best)
        grp = jnp.where(better, g, grp)

    def pick(vals, j):
        out = vals[j]
        for g in range(1, N_EXPERT_GROUPS):
            out = jnp.where(grp == g, vals[g * epg + j], out)
        return out

    s_in = [pick(sel, j) for j in range(epg)]
    a_in = [pick(aff, j) for j in range(epg)]
    i1 = jnp.zeros_like(grp)
    v1, w1 = s_in[0], a_in[0]
    for j in range(1, epg):
        better = s_in[j] > v1
        v1 = jnp.where(better, s_in[j], v1)
        w1 = jnp.where(better, a_in[j], w1)
        i1 = jnp.where(better, j, i1)
    i2 = jnp.full_like(grp, -1)
    v2 = jnp.full_like(v1, -jnp.inf)
    w2 = jnp.zeros_like(w1)
    for j in range(epg):
        better = (i1 != j) & (s_in[j] > v2)
        v2 = jnp.where(better, s_in[j], v2)
        w2 = jnp.where(better, a_in[j], w2)
        i2 = jnp.where(better, j, i2)
    tot = w1 + w2
    eid_ref[0:1, :] = grp * epg + i1
    eid_ref[1:2, :] = grp * epg + i2
    ewt_ref[0:1, :] = w1 / tot
    ewt_ref[1:2, :] = w2 / tot


def _merge(x, gm, att, gate, mod4, row_fn, w, wr_pad, br_col, tm):
    nb, ln, d = x.shape
    nt = ln // tm
    weights = [w["wa"], w["wb"], w["wo"]]

    def tile3(width):
        return pl.BlockSpec((None, tm, width), lambda b, i: (b, i, 0))

    in_specs = [tile3(d), tile3(GM_WIDTH), tile3(N_HEADS * V_HEAD), tile3(2 * d)] + [
        _const_spec(a.shape) for a in weights] + [
        _mod_spec(row_fn, 2, d), _const_spec(w["ln1g"].shape), _const_spec(w["ln1b"].shape),
        _mod_spec(row_fn, 4, d), _mod_spec(row_fn, 3, d), _const_spec(wr_pad.shape), _const_spec(br_col.shape)]
    out_shape = (
        jax.ShapeDtypeStruct((nb, ln, d), F32),
        jax.ShapeDtypeStruct((nb, ln * SUBLANES, d // SUBLANES), F32),
        jax.ShapeDtypeStruct((nb * nt, TOP_K, tm), jnp.int32),
        jax.ShapeDtypeStruct((nb * nt, TOP_K, tm), F32),
    )
    out_specs = (
        tile3(d), pl.BlockSpec((None, tm * SUBLANES, d // SUBLANES), lambda b, i: (b, i, 0)),
        pl.BlockSpec((None, TOP_K, tm), lambda b, i: (b * nt + i, 0, 0)),
        pl.BlockSpec((None, TOP_K, tm), lambda b, i: (b * nt + i, 0, 0)),
    )
    return pl.pallas_call(
        _merge_kernel,
        out_shape=out_shape,
        grid=(nb, nt),
        in_specs=in_specs,
        out_specs=out_specs,
        compiler_params=_params("parallel", "parallel"),
        name="merge",
    )(x, gm, att, gate, *weights, mod4, w["ln1g"], w["ln1b"], mod4, mod4, wr_pad, br_col)


def _load_indices(pos_ref, t, idx_smem, sem):
    cp = pltpu.make_async_copy(pos_ref.at[t], idx_smem, sem)
    cp.start()
    cp.wait()


def _dispatch_kernel(meta_ref, pos_ref, h_ref, buf_hbm, idx_smem, zbuf, sem_i, sem_r, sem_z):
    tm = h_ref.shape[0] // SUBLANES
    tmb = zbuf.shape[0]
    n_tiles = buf_hbm.shape[0] // tmb

    @pl.when(pl.program_id(0) == 0)
    def _():
        zbuf[...] = jnp.zeros_like(zbuf)
        n_valid = meta_ref[2 * N_EXPERTS]
        fills = [(meta_ref[N_EXPERTS + e] > 0, meta_ref[e]) for e in range(N_EXPERTS)]
        fills += [(n_valid + j < n_tiles, n_valid + j) for j in range(N_EXPERTS + 1)]
        copies = [(need, pltpu.make_async_copy(zbuf, buf_hbm.at[pl.ds(pl.multiple_of(tile * tmb, tmb), tmb)], sem_z))
                  for need, tile in fills]
        for need, cp in copies:
            pl.when(need)(cp.start)
        for need, cp in copies:
            pl.when(need)(cp.wait)

    _load_indices(pos_ref, pl.program_id(0), idx_smem, sem_i)

    def body(j, carry):
        for k in range(TOP_K):
            dst = idx_smem[k * tm + j]
            pltpu.make_async_copy(_row_tile(h_ref, j), _row_tile(buf_hbm, dst), sem_r).start(priority=k % 2)
        return carry

    lax.fori_loop(0, tm, body, 0, unroll=DMA_UNROLL)
    for k in range(TOP_K):
        pltpu.make_async_copy(h_ref, buf_hbm.at[pl.ds(0, tm * SUBLANES)], sem_r).wait()


def _dispatch(meta, pos, h2, n_rows, tm):
    n_tok = h2.shape[0] // SUBLANES
    grid_spec = pltpu.PrefetchScalarGridSpec(
        num_scalar_prefetch=1,
        grid=(n_tok // tm,),
        in_specs=[_const_spec(pos.shape), pl.BlockSpec((tm * SUBLANES, LANES), lambda t, meta: (t, 0))],
        out_specs=pl.BlockSpec(memory_space=pl.ANY),
        scratch_shapes=[pltpu.SMEM((TOP_K * tm,), jnp.int32), pltpu.VMEM((MOE_TILE * SUBLANES, LANES), F32),
                        pltpu.SemaphoreType.DMA, pltpu.SemaphoreType.DMA, pltpu.SemaphoreType.DMA],
    )
    return pl.pallas_call(
        _dispatch_kernel,
        out_shape=jax.ShapeDtypeStruct((n_rows * SUBLANES, LANES), F32),
        grid_spec=grid_spec,
        compiler_params=_params("arbitrary"),
        name="dispatch",
    )(meta, pos, h2)


def _ffn_kernel(te_ref, nr_ref, nv_ref, x_ref, wg_ref, wu_ref, wd_ref, o_ref):
    nr = nr_ref[pl.program_id(0)]

    @pl.when(nr > 0)
    def _():
        x = _from_row_tiles(x_ref, x_ref.shape[0] // SUBLANES)
        row = lax.broadcasted_iota(jnp.int32, x.shape, 0)
        x = jnp.where(row < nr, x, 0.0).astype(BF16)
        g = _dot(x, wg_ref[...])
        u = _dot(x, wu_ref[...])
        hm = (g * jax.nn.sigmoid(g) * u).astype(BF16)
        _to_row_tiles(o_ref, _dot(hm, wd_ref[...]))

    @pl.when(nr == 0)
    def _():
        o_ref[...] = jnp.zeros_like(o_ref)


def _ffn(tile_e, tile_rows, n_valid, buf, wg, wu, wd):
    n_tiles = tile_e.shape[0]
    tmb = buf.shape[0] // n_tiles
    d, de = wg.shape[1], wg.shape[2]

    def xmap(t, te, nr, nv):
        return (jnp.minimum(t, nv[0] - 1), 0)

    grid_spec = pltpu.PrefetchScalarGridSpec(
        num_scalar_prefetch=3,
        grid=(n_tiles,),
        in_specs=[
            pl.BlockSpec((tmb, LANES), xmap),
            pl.BlockSpec((None, d, de), lambda t, te, nr, nv: (te[t], 0, 0)),
            pl.BlockSpec((None, d, de), lambda t, te, nr, nv: (te[t], 0, 0)),
            pl.BlockSpec((None, de, d), lambda t, te, nr, nv: (te[t], 0, 0)),
        ],
        out_specs=pl.BlockSpec((tmb, LANES), lambda t, te, nr, nv: (t, 0)),
    )
    return pl.pallas_call(
        _ffn_kernel,
        out_shape=jax.ShapeDtypeStruct((n_tiles * tmb, LANES), F32),
        grid_spec=grid_spec,
        compiler_params=_params("arbitrary"),
        name="ffn",
    )(tile_e, tile_rows, n_valid, buf, wg, wu, wd)


def _combine_kernel(pos_ref, e_hbm, x_ref, wt_ref, g2_ref, lg_ref, lb_ref, o_ref, idx_smem, fbuf, sem_i, sem_r):
    tm = x_ref.shape[0]
    rows = fbuf.shape[0] // SUBLANES
    _load_indices(pos_ref, pl.program_id(0) * pl.num_programs(1) + pl.program_id(1), idx_smem, sem_i)

    def body(i, carry):
        for par in range(2):
            j = 2 * i + par
            pltpu.make_async_copy(_row_tile(e_hbm, idx_smem[j]), _row_tile(fbuf, j), sem_r).start(priority=par)
        return carry

    lax.fori_loop(0, rows // 2, body, 0, unroll=DMA_UNROLL // 2)
    pltpu.make_async_copy(e_hbm.at[pl.ds(0, rows * SUBLANES)], fbuf, sem_r).wait()
    f = wt_ref[:, 0:1] * _from_row_tiles(fbuf, tm) + wt_ref[:, 1:2] * _from_row_tiles(fbuf, tm, row0=tm)
    o_ref[...] = _layer_norm(ALPHA * x_ref[...] + g2_ref[...] * f, lg_ref[...], lb_ref[...])


def _combine(pos, eout, wt, lat, mod4, row_fn, ln_g, ln_b, tm):
    nb, ln, d = lat.shape
    ntb = ln // tm
    return pl.pallas_call(
        _combine_kernel,
        out_shape=jax.ShapeDtypeStruct((nb, ln, d), F32),
        grid=(nb, ntb),
        in_specs=[
            _const_spec(pos.shape),
            pl.BlockSpec(memory_space=pl.ANY),
            pl.BlockSpec((None, tm, d), lambda b, i: (b, i, 0)),
            pl.BlockSpec((None, tm, TOP_K), lambda b, i: (b * ntb + i, 0, 0)),
            _mod_spec(row_fn, 5, d), _const_spec(ln_g.shape), _const_spec(ln_b.shape),
        ],
        out_specs=pl.BlockSpec((None, tm, d), lambda b, i: (b, i, 0)),
        scratch_shapes=[
            pltpu.SMEM((TOP_K * tm,), jnp.int32),
            pltpu.VMEM((TOP_K * tm * SUBLANES, LANES), F32),
            pltpu.SemaphoreType.DMA,
            pltpu.SemaphoreType.DMA,
        ],
        compiler_params=_params("arbitrary", "arbitrary"),
        name="combine",
    )(pos, eout, lat, wt, mod4, ln_g, ln_b)


def _route(eid):
    n_tok_tiles, _, tm = eid.shape
    n_assign = n_tok_tiles * tm * TOP_K
    e_flat = eid.reshape(-1)
    experts = jnp.arange(N_EXPERTS, dtype=jnp.int32)
    onehot = (e_flat[:, None] == experts[None, :]).astype(jnp.int32)
    csum = jnp.cumsum(onehot, axis=0)
    rank = jnp.sum(onehot * csum, axis=1) - 1
    counts = csum[-1]
    tiles_per = (counts + MOE_TILE - 1) // MOE_TILE
    tile_end = jnp.cumsum(tiles_per)
    tile_start = tile_end - tiles_per
    pos = (jnp.sum(onehot * tile_start[None, :], axis=1) * MOE_TILE + rank).astype(jnp.int32)
    n_tiles = -(-(n_assign + N_EXPERTS * (MOE_TILE - 1)) // MOE_TILE)
    t_idx = jnp.arange(n_tiles, dtype=jnp.int32)
    tile_e = jnp.minimum(jnp.sum((t_idx[:, None] >= tile_end[None, :]).astype(jnp.int32), axis=1), N_EXPERTS - 1)
    own = (tile_e[:, None] == experts[None, :]).astype(jnp.int32)
    left = jnp.sum(own * (counts - (t_idx[:, None] - tile_start[None, :]) * MOE_TILE), axis=1)
    tile_rows = jnp.where(t_idx < tile_end[-1], jnp.clip(left, 0, MOE_TILE), 0).astype(jnp.int32)
    n_valid = tile_end[-1:].astype(jnp.int32)
    meta = jnp.concatenate([tile_end - 1, tiles_per, n_valid]).astype(jnp.int32)
    return tile_e.astype(jnp.int32), tile_rows, n_valid, meta, pos.reshape(n_tok_tiles, TOP_K * tm), n_tiles


def _moe(lat, h2, eid, ewt, mod4, row_fn, wg, wu, wd, ln_g, ln_b, tm):
    d = lat.shape[-1]
    tile_e, tile_rows, n_valid, meta, pos, n_tiles = _route(eid)
    buf = _dispatch(meta, pos, h2.reshape(-1, LANES), n_tiles * MOE_TILE, tm)
    eout = _ffn(tile_e, tile_rows, n_valid, buf, wg, wu, wd)
    return _combine(pos, eout, ewt.transpose(0, 2, 1), lat, mod4, row_fn, ln_g, ln_b, tm)


def _swap_halves(a):
    h = a.shape[-1] // 2
    return jnp.concatenate([a[..., h:], a[..., :h]], axis=-1)


def _layer_weights(l, w_in, gm_ln_g, gm_ln_b, gm_ws, gm_bs, q_norm, w_uq, kv_norm, w_ukv, w_proj_a, w_proj_b,
                   b_merge, w_out, ln1_g, ln1_b):
    wi = w_in[l]
    kr = wi[:, OFF_KR:OFF_G]
    uq = w_uq[l].reshape(Q_LORA, N_HEADS, QK_NOPE + QK_ROPE)
    uq_r = uq[..., QK_NOPE:]
    uq_ext = jnp.concatenate([uq[..., :QK_NOPE], uq_r, _swap_halves(uq_r)], axis=-1)
    return dict(
        wu=wi[:, 0:GM_WIDTH].astype(BF16),
        wv=wi[:, GM_WIDTH:2 * GM_WIDTH].astype(BF16),
        wq=wi[:, OFF_Q:OFF_KV].astype(BF16),
        wkv=wi[:, OFF_KV:OFF_KR].astype(BF16),
        wkr=jnp.concatenate([kr, _swap_halves(kr)], axis=-1).astype(BF16),
        wg=wi[:, OFF_G:].astype(BF16),
        lng=gm_ln_g[l].reshape(1, -1), lnb=gm_ln_b[l].reshape(1, -1),
        ws=gm_ws[l].astype(BF16), bs=gm_bs[l].reshape(GM_GROUPS, CHUNK, 1),
        qn=q_norm[l].reshape(1, -1), wuq=uq_ext.reshape(Q_LORA, N_HEADS * QK_PAD).astype(BF16),
        kvn=kv_norm[l].reshape(1, -1), wukv=w_ukv[l].astype(BF16),
        bm=b_merge[l].reshape(1, -1),
        wa=w_proj_a[l].astype(BF16), wb=w_proj_b[l].astype(BF16), wo=w_out[l].astype(BF16),
        ln1g=ln1_g[l].reshape(1, -1), ln1b=ln1_b[l].reshape(1, -1),
    )


def _rope_table(seq):
    rows = seq // GRID_W
    r = jnp.repeat(jnp.arange(rows), GRID_W).astype(F32)
    c = jnp.tile(jnp.arange(GRID_W), rows).astype(F32)
    n_freq = QK_ROPE // 4
    inv = ROPE_THETA ** (-jnp.arange(n_freq, dtype=F32) / n_freq)
    ang = jnp.concatenate([r[:, None] * inv, c[:, None] * inv], axis=-1)
    cos, sin = jnp.cos(ang), jnp.sin(ang)
    return jnp.concatenate([cos, cos, -sin, sin], axis=-1)


def kernel(x, c, ctx, c_ctx, w_ada, b_ada, w_in, gm_ln_g, gm_ln_b, gm_ws, gm_bs, q_norm, w_uq, kv_norm, w_ukv,
           w_proj_a, w_proj_b, b_merge, w_out, ln1_g, ln1_b, w_router, b_router, w_e_gate, w_e_up, w_e_down,
           ln2_g, ln2_b):
    nb, seq, d = x.shape
    depth = w_ada.shape[0]
    tm = ROW_TILE
    tq = 2 * ROW_TILE if seq % (2 * ROW_TILE) == 0 else ROW_TILE

    mod_rows = -(-(nb + 1) // 8) * 8
    cc = jnp.concatenate([c, c_ctx[None, :], jnp.zeros((mod_rows - nb - 1, d), F32)], axis=0)
    mod = _ada(cc, w_ada, b_ada)
    tab = _rope_table(seq)
    wr_hi = w_router.astype(BF16)
    wr_lo = (w_router - wr_hi.astype(F32)).astype(BF16)
    lane_pad = jnp.zeros((d, LANES - N_EXPERTS), BF16)
    wr_pad = jnp.concatenate([wr_hi, lane_pad, wr_lo, lane_pad], axis=1)
    br_col = b_router.reshape(N_EXPERTS, 1)

    def lat_row(b):
        return b

    def ctx_row(b):
        return nb

    lat, cx = x, ctx
    for l in range(depth):
        last = l == depth - 1
        w = _layer_weights(l, w_in, gm_ln_g, gm_ln_b, gm_ws, gm_bs, q_norm, w_uq, kv_norm, w_ukv,
                           w_proj_a, w_proj_b, b_merge, w_out, ln1_g, ln1_b)
        wg, wu, wd = w_e_gate[l].astype(BF16), w_e_up[l].astype(BF16), w_e_down[l].astype(BF16)
        ln2 = (ln2_g[l].reshape(1, -1), ln2_b[l].reshape(1, -1))
        mod4 = mod[l].reshape(mod_rows, 6, 1, d)

        if last:
            ck, cv = _ctx_kv(cx, mod4, ctx_row, w, tm)
        else:
            cgm, cq, ck, cv, cgate = _inproj(cx, mod4, ctx_row, w, None, tm)
        gm, q, k, v, gate = _inproj(lat, mod4, lat_row, w, tab, tq)
        att = (_attention if seq % (2 * tq) == 0 else _attention_simple)(q, [(ck, cv), (k, v)], tq)
        lat1, h2, eid, ewt = _merge(lat, gm, att, gate, mod4, lat_row, w, wr_pad, br_col, tq)
        lat = _moe(lat1, h2, eid, ewt, mod4, lat_row, wg, wu, wd, *ln2, tq)
        if not last:
            catt = _attention_simple(cq, [(ck, cv)], tm)
            cx1, ch2, ceid, cewt = _merge(cx, cgm, catt, cgate, mod4, ctx_row, w, wr_pad, br_col, tm)
            cx = _moe(cx1, ch2, ceid, cewt, mod4, ctx_row, wg, wu, wd, *ln2, tm)
    return lat
```

```python
import functools
import math

import jax
import jax.numpy as jnp
from jax import lax
from jax.experimental import pallas as pl
from jax.experimental.pallas import tpu as pltpu

F32 = jnp.float32
BF16 = jnp.bfloat16

D_MODEL = 1024
DEPTH = 2
GRID_W = 64
CHUNK = 128
GM_GROUPS = 8
GM_WIDTH = 1024
N_HEADS = 8
Q_LORA = 384
KV_LORA = 256
QK_NOPE = 128
QK_ROPE = 64
V_HEAD = 128
ROPE_THETA = 10000.0
N_EXPERTS = 16
N_EXPERT_GROUPS = 4
EXPERTS_PER_GROUP = N_EXPERTS // N_EXPERT_GROUPS
TOP_K = 2
D_EXPERT = 512
LN_EPS = 1e-5
ALPHA = (2.0 * DEPTH) ** 0.25

OFF_Q = 2 * GM_WIDTH
OFF_KV = OFF_Q + Q_LORA
OFF_KR = OFF_KV + KV_LORA
OFF_G = OFF_KR + QK_ROPE

LANES = 128
QK_PAD = 2 * LANES
V_PAD = 2 * LANES
ROW_TILE = 256
MOE_TILE = 512
DMA_UNROLL = 8
SUBLANES = 8
VMEM_LIMIT = 56 * 1024 * 1024

HIGHEST = lax.Precision.HIGHEST
LOG2E = 1.4426950408889634


def _dot(a, b):
    return jnp.dot(a, b, preferred_element_type=F32)


def _dot_nt(a, b):
    return lax.dot_general(a, b, (((1,), (1,)), ((), ())), preferred_element_type=F32)


def _layer_norm(z, g, b):
    mu = jnp.mean(z, axis=-1, keepdims=True)
    zc = z - mu
    var = jnp.mean(zc * zc, axis=-1, keepdims=True)
    return zc * lax.rsqrt(var + LN_EPS) * g + b


def _rms_norm(z, g):
    return z * lax.rsqrt(jnp.mean(z * z, axis=-1, keepdims=True) + LN_EPS) * g


def _to_row_tiles(ref, val, row0=0):
    rows, d = val.shape
    for c in range(d // LANES):
        ref[pl.ds(row0 * SUBLANES + c, rows, stride=SUBLANES), :] = val[:, c * LANES:(c + 1) * LANES]


def _from_row_tiles(ref, rows, row0=0):
    return jnp.concatenate(
        [ref[pl.ds(row0 * SUBLANES + c, rows, stride=SUBLANES), :] for c in range(SUBLANES)], axis=-1)


def _row_tile(ref, r):
    return ref.at[pl.ds(pl.multiple_of(r * SUBLANES, SUBLANES), SUBLANES)]


def _params(*sem):
    return pltpu.CompilerParams(dimension_semantics=sem, vmem_limit_bytes=VMEM_LIMIT)


def _const_spec(shape):
    nd = len(shape)
    return pl.BlockSpec(shape, lambda *_: (0,) * nd, pipeline_mode=pl.Buffered(1))


def _mod_spec(row_fn, k, d):
    return pl.BlockSpec((None, None, 1, d), lambda b, i: (row_fn(b), k, 0, 0))


def _ada_kernel(c_ref, w_ref, b_ref, o_ref):
    cc = c_ref[...]
    s = cc * jax.nn.sigmoid(cc)
    o_ref[...] = jnp.dot(s, w_ref[...], precision=HIGHEST, preferred_element_type=F32) + b_ref[...]


def _ada(cc, w_ada, b_ada):
    depth, d, n = w_ada.shape
    rows = cc.shape[0]
    tn = 1536
    return pl.pallas_call(
        _ada_kernel,
        out_shape=jax.ShapeDtypeStruct((depth, rows, n), F32),
        grid=(depth, n // tn),
        in_specs=[
            pl.BlockSpec((rows, d), lambda l, j: (0, 0)),
            pl.BlockSpec((None, d, tn), lambda l, j: (l, 0, j)),
            pl.BlockSpec((None, 1, tn), lambda l, j: (l, 0, j)),
        ],
        out_specs=pl.BlockSpec((None, rows, tn), lambda l, j: (l, 0, j)),
        compiler_params=_params("parallel", "parallel"),
        name="ada",
    )(cc, w_ada, b_ada.reshape(depth, 1, n))


def _rope(r, tab):
    m = r * tab
    return m + pltpu.roll(m, LANES // 2, axis=1)


def _write_kv(h, tab, wkv_ref, wkr_ref, kvn_ref, wukv_ref, k_ref, v_ref):
    tm = h.shape[0]
    kvn = _rms_norm(_dot(h, wkv_ref[...]), kvn_ref[...]).astype(BF16)
    pkv = _dot(kvn, wukv_ref[...])
    kr = _dot(h, wkr_ref[...])
    if tab is not None:
        kr = _rope(kr, tab)
    lane = lax.broadcasted_iota(jnp.int32, (tm, LANES), 1)
    kr = jnp.where(lane < LANES // 2, kr, 0.0).astype(BF16)
    ones = jnp.ones((tm, V_PAD - V_HEAD), BF16)
    for hh in range(N_HEADS):
        o = hh * (QK_NOPE + V_HEAD)
        k_ref[hh, :, 0:QK_NOPE] = pkv[:, o:o + QK_NOPE].astype(BF16)
        k_ref[hh, :, QK_NOPE:QK_PAD] = kr
        v_ref[hh, :, 0:V_HEAD] = pkv[:, o + QK_NOPE:o + QK_NOPE + V_HEAD].astype(BF16)
        v_ref[hh, :, V_HEAD:V_PAD] = ones


def _inproj_kernel(*refs, use_rope):
    if use_rope:
        (x_ref, sc_ref, sh_ref, wu_ref, wv_ref, wq_ref, wkv_ref, wkr_ref, wg_ref, lng_ref, lnb_ref, ws_ref,
         bs_ref, qn_ref, wuq_ref, kvn_ref, wukv_ref, bm_ref, tab_ref, gm_ref, q_ref, k_ref, v_ref, gate_ref) = refs
        tab = tab_ref[...]
    else:
        (x_ref, sc_ref, sh_ref, wu_ref, wv_ref, wq_ref, wkv_ref, wkr_ref, wg_ref, lng_ref, lnb_ref, ws_ref,
         bs_ref, qn_ref, wuq_ref, kvn_ref, wukv_ref, bm_ref, gm_ref, q_ref, k_ref, v_ref, gate_ref) = refs
        tab = None
    tm = x_ref.shape[0]
    h = (x_ref[...] * (1.0 + sc_ref[...]) + sh_ref[...]).astype(BF16)

    u = jax.nn.gelu(_dot(h, wu_ref[...]))
    v = jax.nn.gelu(_dot(h, wv_ref[...]))
    vn = _layer_norm(v, lng_ref[...], lnb_ref[...]).astype(BF16)
    gd = GM_WIDTH // GM_GROUPS
    for c in range(tm // CHUNK):
        rs = slice(c * CHUNK, (c + 1) * CHUNK)
        for g in range(GM_GROUPS):
            cs = slice(g * gd, (g + 1) * gd)
            s = _dot(ws_ref[g], vn[rs, cs]) + bs_ref[g]
            gm_ref[rs, cs] = (u[rs, cs] * s).astype(BF16)

    scale = (QK_NOPE + QK_ROPE) ** -0.5 * LOG2E
    qn = _rms_norm(_dot(h, wq_ref[...]), qn_ref[...] * scale).astype(BF16)
    pq = _dot(qn, wuq_ref[...])
    for hh in range(N_HEADS):
        o = hh * QK_PAD
        q_ref[hh, :, 0:QK_NOPE] = pq[:, o:o + QK_NOPE].astype(BF16)
        qr = pq[:, o + QK_NOPE:o + QK_PAD]
        q_ref[hh, :, QK_NOPE:QK_PAD] = (qr if tab is None else _rope(qr, tab)).astype(BF16)

    _write_kv(h, tab, wkv_ref, wkr_ref, kvn_ref, wukv_ref, k_ref, v_ref)
    gate_ref[...] = jax.nn.sigmoid(_dot(h, wg_ref[...]) + bm_ref[...]).astype(BF16)


def _inproj(x, mod4, row_fn, w, tab, tm):
    nb, ln, d = x.shape
    weights = [w["wu"], w["wv"], w["wq"], w["wkv"], w["wkr"], w["wg"], w["lng"], w["lnb"], w["ws"],
               w["bs"], w["qn"], w["wuq"], w["kvn"], w["wukv"], w["bm"]]
    in_specs = [
        pl.BlockSpec((None, tm, d), lambda b, i: (b, i, 0)),
        _mod_spec(row_fn, 1, d), _mod_spec(row_fn, 0, d),
    ] + [_const_spec(a.shape) for a in weights]
    args = [x, mod4, mod4] + weights
    if tab is not None:
        in_specs.append(pl.BlockSpec((tm, LANES), lambda b, i: (i, 0)))
        args.append(tab)
    out_shape = (
        jax.ShapeDtypeStruct((nb, ln, GM_WIDTH), BF16),
        jax.ShapeDtypeStruct((nb, N_HEADS, ln, QK_PAD), BF16),
        jax.ShapeDtypeStruct((nb, N_HEADS, ln, QK_PAD), BF16),
        jax.ShapeDtypeStruct((nb, N_HEADS, ln, V_PAD), BF16),
        jax.ShapeDtypeStruct((nb, ln, 2 * d), BF16),
    )
    out_specs = (
        pl.BlockSpec((None, tm, GM_WIDTH), lambda b, i: (b, i, 0)),
        pl.BlockSpec((None, N_HEADS, tm, QK_PAD), lambda b, i: (b, 0, i, 0)),
        pl.BlockSpec((None, N_HEADS, tm, QK_PAD), lambda b, i: (b, 0, i, 0)),
        pl.BlockSpec((None, N_HEADS, tm, V_PAD), lambda b, i: (b, 0, i, 0)),
        pl.BlockSpec((None, tm, 2 * d), lambda b, i: (b, i, 0)),
    )
    return pl.pallas_call(
        functools.partial(_inproj_kernel, use_rope=tab is not None),
        out_shape=out_shape,
        grid=(nb, ln // tm),
        in_specs=in_specs,
        out_specs=out_specs,
        compiler_params=_params("parallel", "parallel"),
        name="inproj",
    )(*args)


def _ctx_kv_kernel(x_ref, sc_ref, sh_ref, wkv_ref, wkr_ref, kvn_ref, wukv_ref, k_ref, v_ref):
    h = (x_ref[...] * (1.0 + sc_ref[...]) + sh_ref[...]).astype(BF16)
    _write_kv(h, None, wkv_ref, wkr_ref, kvn_ref, wukv_ref, k_ref, v_ref)


def _ctx_kv(x, mod4, row_fn, w, tm):
    nb, ln, d = x.shape
    weights = [w["wkv"], w["wkr"], w["kvn"], w["wukv"]]
    return pl.pallas_call(
        _ctx_kv_kernel,
        out_shape=(jax.ShapeDtypeStruct((nb, N_HEADS, ln, QK_PAD), BF16),
                   jax.ShapeDtypeStruct((nb, N_HEADS, ln, V_PAD), BF16)),
        grid=(nb, ln // tm),
        in_specs=[pl.BlockSpec((None, tm, d), lambda b, i: (b, i, 0)),
                  _mod_spec(row_fn, 1, d), _mod_spec(row_fn, 0, d)] + [_const_spec(a.shape) for a in weights],
        out_specs=(pl.BlockSpec((None, N_HEADS, tm, QK_PAD), lambda b, i: (b, 0, i, 0)),
                   pl.BlockSpec((None, N_HEADS, tm, V_PAD), lambda b, i: (b, 0, i, 0))),
        compiler_params=_params("parallel", "parallel"),
        name="ctx_kv",
    )(x, mod4, mod4, *weights)


def _attn_simple_kernel(*refs, n_kv, sub):
    q_ref, o_ref = refs[0], refs[-1]
    kv = [(refs[1 + 2 * j], refs[2 + 2 * j]) for j in range(n_kv)]
    for r0 in range(0, q_ref.shape[0], sub):
        q = q_ref[r0:r0 + sub, :]
        s = [_dot_nt(q, k_ref[...]) for k_ref, _ in kv]
        m = s[0].max(axis=-1, keepdims=True)
        for sj in s[1:]:
            m = jnp.maximum(m, sj.max(axis=-1, keepdims=True))
        acc = None
        for sj, (_, v_ref) in zip(s, kv):
            p = jnp.exp2((sj - m).astype(BF16))
            pv = _dot(p, v_ref[...])
            acc = pv if acc is None else acc + pv
        o_ref[r0:r0 + sub, :] = (acc[:, 0:V_HEAD] / acc[:, V_HEAD:V_PAD]).astype(BF16)


def _attention_simple(q, kvs, tq):
    nb, nh, lq, _ = q.shape
    in_specs = [pl.BlockSpec((None, None, tq, QK_PAD), lambda b, h, i: (b, h, i, 0))]
    args = [q]
    for k, v in kvs:
        lk = k.shape[2]
        in_specs.append(pl.BlockSpec((None, None, lk, QK_PAD), lambda b, h, i: (b, h, 0, 0)))
        in_specs.append(pl.BlockSpec((None, None, lk, V_PAD), lambda b, h, i: (b, h, 0, 0)))
        args += [k, v]
    return pl.pallas_call(
        functools.partial(_attn_simple_kernel, n_kv=len(kvs), sub=min(tq, ROW_TILE)),
        out_shape=jax.ShapeDtypeStruct((nb, lq, nh * V_HEAD), BF16),
        grid=(nb, nh, lq // tq),
        in_specs=in_specs,
        out_specs=pl.BlockSpec((None, tq, V_HEAD), lambda b, h, i: (b, i, h)),
        compiler_params=_params("parallel", "parallel", "parallel"),
        name="attn_ctx",
    )(*args)


def _attn_pipe_kernel(*refs, n_kv, tq):
    q_ref = refs[0]
    k_refs = refs[1:1 + n_kv]
    v_refs = refs[1 + n_kv:1 + 2 * n_kv]
    o_ref = refs[1 + 2 * n_kv]
    scratch = refs[2 + 2 * n_kv:]
    s_bufs = [scratch[2 * j:2 * j + 2] for j in range(n_kv)]
    p_bufs = [scratch[2 * n_kv + 2 * j:2 * n_kv + 2 * j + 2] for j in range(n_kv)]

    @pl.when(pl.program_id(0) == 0)
    def _():
        for s_buf, p_buf in zip(s_bufs, p_bufs):
            s_buf[1][...] = jnp.zeros(s_buf[1].shape, F32)
            p_buf[0][...] = jnp.ones(p_buf[0].shape, BF16)

    for par in range(2):
        rows = slice(par * tq, (par + 1) * tq)
        m = None
        for s_buf in s_bufs:
            mj = s_buf[1 - par][...].max(axis=-1, keepdims=True)
            m = mj if m is None else jnp.maximum(m, mj)
        for s_buf, p_buf in zip(s_bufs, p_bufs):
            p_buf[1 - par][...] = jnp.exp2((s_buf[1 - par][...] - m).astype(BF16))
        q = q_ref[rows, :]
        for k_ref, s_buf in zip(k_refs, s_bufs):
            s_buf[par][...] = _dot_nt(q, k_ref[...])
        acc = None
        for p_buf, v_ref in zip(p_bufs, v_refs):
            pv = _dot(p_buf[par][...], v_ref[...])
            acc = pv if acc is None else acc + pv
        o_ref[rows, :] = (acc[:, 0:V_HEAD] / acc[:, V_HEAD:V_PAD]).astype(BF16)


def _attention(q, kvs, tq):
    nb, nh, lq, _ = q.shape
    npair = lq // (2 * tq)
    n_steps = nb * nh * npair

    def cur(g):
        t = jnp.minimum(g, n_steps - 1)
        return t // (nh * npair), (t // npair) % nh, t % npair

    def prev(g):
        return cur(jnp.maximum(g - 1, 0))

    def kv_spec(arr, which):
        def index_map(g):
            b, h, _ = which(g)
            return (b, h, 0, 0)
        return pl.BlockSpec((None, None) + arr.shape[2:], index_map)

    def q_map(g):
        b, h, i = cur(g)
        return (b, h, i, 0)

    def o_map(g):
        b, h, i = prev(g)
        return (b, i, h)

    in_specs = [pl.BlockSpec((None, None, 2 * tq, QK_PAD), q_map)]
    in_specs += [kv_spec(k, cur) for k, _ in kvs] + [kv_spec(v, prev) for _, v in kvs]
    return pl.pallas_call(
        functools.partial(_attn_pipe_kernel, n_kv=len(kvs), tq=tq),
        out_shape=jax.ShapeDtypeStruct((nb, lq, nh * V_HEAD), BF16),
        grid=(n_steps + 1,),
        in_specs=in_specs,
        out_specs=pl.BlockSpec((None, 2 * tq, V_HEAD), o_map),
        scratch_shapes=[pltpu.VMEM((tq, k.shape[2]), F32) for k, _ in kvs for _ in range(2)]
        + [pltpu.VMEM((tq, k.shape[2]), BF16) for k, _ in kvs for _ in range(2)],
        compiler_params=_params("arbitrary"),
        name="attn",
    )(q, *[k for k, _ in kvs], *[v for _, v in kvs])


def _top2_of4(a, b, c, d):
    m1, n1 = jnp.maximum(a, b), jnp.minimum(a, b)
    m2, n2 = jnp.maximum(c, d), jnp.minimum(c, d)
    return jnp.maximum(m1, m2) + jnp.maximum(jnp.minimum(m1, m2), jnp.maximum(n1, n2))


def _merge_kernel(x_ref, gm_ref, att_ref, gate_ref, wa_ref, wb_ref, wo_ref, g1_ref, lg_ref, lb_ref,
                  sc2_ref, sh2_ref, wr_ref, br_ref, lat_ref, h2_ref, eid_ref, ewt_ref):
    d = x_ref.shape[1]
    ya = _dot(gm_ref[...], wa_ref[...])
    yb = _dot(att_ref[...], wb_ref[...])
    y = gate_ref[:, 0:d].astype(F32) * ya + gate_ref[:, d:2 * d].astype(F32) * yb
    mix = _dot(y.astype(BF16), wo_ref[...])
    lat = _layer_norm(ALPHA * x_ref[...] + g1_ref[...] * mix, lg_ref[...], lb_ref[...])
    lat_ref[...] = lat
    h2 = lat * (1.0 + sc2_ref[...]) + sh2_ref[...]
    _to_row_tiles(h2_ref, h2)

    h2_hi = h2.astype(BF16)
    h2_lo = (h2 - h2_hi.astype(F32)).astype(BF16)
    l_hi = _dot(h2_hi, wr_ref[...])
    logits = l_hi[:, 0:LANES] + l_hi[:, LANES:2 * LANES] + _dot(h2_lo, wr_ref[:, 0:LANES])
    aff_t = jax.nn.sigmoid(logits.T[0:N_EXPERTS, :])
    sel_t = aff_t + br_ref[...]
    aff = [aff_t[e:e + 1, :] for e in range(N_EXPERTS)]
    sel = [sel_t[e:e + 1, :] for e in range(N_EXPERTS)]
    epg = EXPERTS_PER_GROUP
    best = _top2_of4(*sel[0:epg])
    grp = jnp.zeros_like(best, dtype=jnp.int32)
    for g in range(1, N_EXPERT_GROUPS):
        sc = _top2_of4(*sel[g * epg:(g + 1) * epg])
        better = sc > best
        best = jnp.where(better, sc, best)
        grp = jnp.where(better, g, grp)

    def pick(vals, j):
        out = vals[j]
        for g in range(1, N_EXPERT_GROUPS):
            out = jnp.where(grp == g, vals[g * epg + j], out)
        return out

    s_in = [pick(sel, j) for j in range(epg)]
    a_in = [pick(aff, j) for j in range(epg)]
    i1 = jnp.zeros_like(grp)
    v1, w1 = s_in[0], a_in[0]
    for j in range(1, epg):
        better = s_in[j] > v1
        v1 = jnp.where(better, s_in[j], v1)
        w1 = jnp.where(better, a_in[j], w1)
        i1 = jnp.where(better, j, i1)
    i2 = jnp.full_like(grp, -1)
    v2 = jnp.full_like(v1, -jnp.inf)
    w2 = jnp.zeros_like(w1)
    for j in range(epg):
        better = (i1 != j) & (s_in[j] > v2)
        v2 = jnp.where(better, s_in[j], v2)
        w2 = jnp.where(better, a_in[j], w2)
        i2 = jnp.where(better, j, i2)
    tot = w1 + w2
    eid_ref[0:1, :] = grp * epg + i1
    eid_ref[1:2, :] = grp * epg + i2
    ewt_ref[0:1, :] = w1 / tot
    ewt_ref[1:2, :] = w2 / tot


def _merge(x, gm, att, gate, mod4, row_fn, w, wr_pad, br_col, tm):
    nb, ln, d = x.shape
    nt = ln // tm
    weights = [w["wa"], w["wb"], w["wo"]]

    def tile3(width):
        return pl.BlockSpec((None, tm, width), lambda b, i: (b, i, 0))

    in_specs = [tile3(d), tile3(GM_WIDTH), tile3(N_HEADS * V_HEAD), tile3(2 * d)] + [
        _const_spec(a.shape) for a in weights] + [
        _mod_spec(row_fn, 2, d), _const_spec(w["ln1g"].shape), _const_spec(w["ln1b"].shape),
        _mod_spec(row_fn, 4, d), _mod_spec(row_fn, 3, d), _const_spec(wr_pad.shape), _const_spec(br_col.shape)]
    out_shape = (
        jax.ShapeDtypeStruct((nb, ln, d), F32),
        jax.ShapeDtypeStruct((nb, ln * SUBLANES, d // SUBLANES), F32),
        jax.ShapeDtypeStruct((nb * nt, TOP_K, tm), jnp.int32),
        jax.ShapeDtypeStruct((nb * nt, TOP_K, tm), F32),
    )
    out_specs = (
        tile3(d), pl.BlockSpec((None, tm * SUBLANES, d // SUBLANES), lambda b, i: (b, i, 0)),
        pl.BlockSpec((None, TOP_K, tm), lambda b, i: (b * nt + i, 0, 0)),
        pl.BlockSpec((None, TOP_K, tm), lambda b, i: (b * nt + i, 0, 0)),
    )
    return pl.pallas_call(
        _merge_kernel,
        out_shape=out_shape,
        grid=(nb, nt),
        in_specs=in_specs,
        out_specs=out_specs,
        compiler_params=_params("parallel", "parallel"),
        name="merge",
    )(x, gm, att, gate, *weights, mod4, w["ln1g"], w["ln1b"], mod4, mod4, wr_pad, br_col)


def _load_indices(pos_ref, t, idx_smem, sem):
    cp = pltpu.make_async_copy(pos_ref.at[t], idx_smem, sem)
    cp.start()
    cp.wait()


def _dispatch_kernel(meta_ref, pos_ref, h_ref, buf_hbm, idx_smem, zbuf, sem_i, sem_r, sem_z):
    tm = h_ref.shape[0] // SUBLANES
    tmb = zbuf.shape[0]
    n_tiles = buf_hbm.shape[0] // tmb

    @pl.when(pl.program_id(0) == 0)
    def _():
        zbuf[...] = jnp.zeros_like(zbuf)
        n_valid = meta_ref[2 * N_EXPERTS]
        fills = [(meta_ref[N_EXPERTS + e] > 0, meta_ref[e]) for e in range(N_EXPERTS)]
        fills += [(n_valid + j < n_tiles, n_valid + j) for j in range(N_EXPERTS + 1)]
        copies = [(need, pltpu.make_async_copy(zbuf, buf_hbm.at[pl.ds(pl.multiple_of(tile * tmb, tmb), tmb)], sem_z))
                  for need, tile in fills]
        for need, cp in copies:
            pl.when(need)(cp.start)
        for need, cp in copies:
            pl.when(need)(cp.wait)

    _load_indices(pos_ref, pl.program_id(0), idx_smem, sem_i)

    def body(j, carry):
        for k in range(TOP_K):
            dst = idx_smem[k * tm + j]
            pltpu.make_async_copy(_row_tile(h_ref, j), _row_tile(buf_hbm, dst), sem_r).start(priority=k % 2)
        return carry

    lax.fori_loop(0, tm, body, 0, unroll=DMA_UNROLL)
    for k in range(TOP_K):
        pltpu.make_async_copy(h_ref, buf_hbm.at[pl.ds(0, tm * SUBLANES)], sem_r).wait()


def _dispatch(meta, pos, h2, n_rows, tm):
    n_tok = h2.shape[0] // SUBLANES
    grid_spec = pltpu.PrefetchScalarGridSpec(
        num_scalar_prefetch=1,
        grid=(n_tok // tm,),
        in_specs=[_const_spec(pos.shape), pl.BlockSpec((tm * SUBLANES, LANES), lambda t, meta: (t, 0))],
        out_specs=pl.BlockSpec(memory_space=pl.ANY),
        scratch_shapes=[pltpu.SMEM((TOP_K * tm,), jnp.int32), pltpu.VMEM((MOE_TILE * SUBLANES, LANES), F32),
                        pltpu.SemaphoreType.DMA, pltpu.SemaphoreType.DMA, pltpu.SemaphoreType.DMA],
    )
    return pl.pallas_call(
        _dispatch_kernel,
        out_shape=jax.ShapeDtypeStruct((n_rows * SUBLANES, LANES), F32),
        grid_spec=grid_spec,
        compiler_params=_params("arbitrary"),
        name="dispatch",
    )(meta, pos, h2)


def _ffn_kernel(te_ref, nr_ref, nv_ref, x_ref, wg_ref, wu_ref, wd_ref, o_ref):
    nr = nr_ref[pl.program_id(0)]

    @pl.when(nr > 0)
    def _():
        x = _from_row_tiles(x_ref, x_ref.shape[0] // SUBLANES)
        row = lax.broadcasted_iota(jnp.int32, x.shape, 0)
        x = jnp.where(row < nr, x, 0.0).astype(BF16)
        g = _dot(x, wg_ref[...])
        u = _dot(x, wu_ref[...])
        hm = (g * jax.nn.sigmoid(g) * u).astype(BF16)
        _to_row_tiles(o_ref, _dot(hm, wd_ref[...]))

    @pl.when(nr == 0)
    def _():
        o_ref[...] = jnp.zeros_like(o_ref)


def _ffn(tile_e, tile_rows, n_valid, buf, wg, wu, wd):
    n_tiles = tile_e.shape[0]
    tmb = buf.shape[0] // n_tiles
    d, de = wg.shape[1], wg.shape[2]

    def xmap(t, te, nr, nv):
        return (jnp.minimum(t, nv[0] - 1), 0)

    grid_spec = pltpu.PrefetchScalarGridSpec(
        num_scalar_prefetch=3,
        grid=(n_tiles,),
        in_specs=[
            pl.BlockSpec((tmb, LANES), xmap),
            pl.BlockSpec((None, d, de), lambda t, te, nr, nv: (te[t], 0, 0)),
            pl.BlockSpec((None, d, de), lambda t, te, nr, nv: (te[t], 0, 0)),
            pl.BlockSpec((None, de, d), lambda t, te, nr, nv: (te[t], 0, 0)),
        ],
        out_specs=pl.BlockSpec((tmb, LANES), lambda t, te, nr, nv: (t, 0)),
    )
    return pl.pallas_call(
        _ffn_kernel,
        out_shape=jax.ShapeDtypeStruct((n_tiles * tmb, LANES), F32),
        grid_spec=grid_spec,
        compiler_params=_params("arbitrary"),
        name="ffn",
    )(tile_e, tile_rows, n_valid, buf, wg, wu, wd)


def _combine_kernel(pos_ref, e_hbm, x_ref, wt_ref, g2_ref, lg_ref, lb_ref, o_ref, idx_smem, fbuf, sem_i, sem_r):
    tm = x_ref.shape[0]
    rows = fbuf.shape[1] // SUBLANES
    t = pl.program_id(0) * pl.num_programs(1) + pl.program_id(1)
    n_steps = pl.num_programs(0) * pl.num_programs(1)
    slot = lax.rem(t, 2)

    def fetch(tt, s):
        _load_indices(pos_ref, tt, idx_smem, sem_i)

        def body(i, carry):
            for par in range(2):
                j = 2 * i + par
                pltpu.make_async_copy(_row_tile(e_hbm, idx_smem[j]), _row_tile(fbuf.at[s], j),
                                      sem_r.at[s]).start(priority=par)
            return carry

        lax.fori_loop(0, rows // 2, body, 0, unroll=DMA_UNROLL // 2)

    @pl.when(t == 0)
    def _():
        fetch(0, 0)

    @pl.when(t + 1 < n_steps)
    def _():
        fetch(t + 1, 1 - slot)

    for s in range(2):
        @pl.when(slot == s)
        def _():
            buf = fbuf.at[s]
            pltpu.make_async_copy(e_hbm.at[pl.ds(0, rows * SUBLANES)], buf, sem_r.at[s]).wait()
            f = wt_ref[:, 0:1] * _from_row_tiles(buf, tm) + wt_ref[:, 1:2] * _from_row_tiles(buf, tm, row0=tm)
            o_ref[...] = _layer_norm(ALPHA * x_ref[...] + g2_ref[...] * f, lg_ref[...], lb_ref[...])


def _combine(pos, eout, wt, lat, mod4, row_fn, ln_g, ln_b, tm):
    nb, ln, d = lat.shape
    ntb = ln // tm
    return pl.pallas_call(
        _combine_kernel,
        out_shape=jax.ShapeDtypeStruct((nb, ln, d), F32),
        grid=(nb, ntb),
        in_specs=[
            _const_spec(pos.shape),
            pl.BlockSpec(memory_space=pl.ANY),
            pl.BlockSpec((None, tm, d), lambda b, i: (b, i, 0)),
            pl.BlockSpec((None, tm, TOP_K), lambda b, i: (b * ntb + i, 0, 0)),
            _mod_spec(row_fn, 5, d), _const_spec(ln_g.shape), _const_spec(ln_b.shape),
        ],
        out_specs=pl.BlockSpec((None, tm, d), lambda b, i: (b, i, 0)),
        scratch_shapes=[
            pltpu.SMEM((TOP_K * tm,), jnp.int32),
            pltpu.VMEM((2, TOP_K * tm * SUBLANES, LANES), F32),
            pltpu.SemaphoreType.DMA,
            pltpu.SemaphoreType.DMA((2,)),
        ],
        compiler_params=_params("arbitrary", "arbitrary"),
        name="combine",
    )(pos, eout, lat, wt, mod4, ln_g, ln_b)


def _route(eid):
    n_tok_tiles, _, tm = eid.shape
    n_assign = n_tok_tiles * tm * TOP_K
    e_flat = eid.reshape(-1)
    experts = jnp.arange(N_EXPERTS, dtype=jnp.int32)
    onehot = (e_flat[:, None] == experts[None, :]).astype(jnp.int32)
    csum = jnp.cumsum(onehot, axis=0)
    rank = jnp.sum(onehot * csum, axis=1) - 1
    counts = csum[-1]
    tiles_per = (counts + MOE_TILE - 1) // MOE_TILE
    tile_end = jnp.cumsum(tiles_per)
    tile_start = tile_end - tiles_per
    pos = (jnp.sum(onehot * tile_start[None, :], axis=1) * MOE_TILE + rank).astype(jnp.int32)
    n_tiles = -(-(n_assign + N_EXPERTS * (MOE_TILE - 1)) // MOE_TILE)
    t_idx = jnp.arange(n_tiles, dtype=jnp.int32)
    tile_e = jnp.minimum(jnp.sum((t_idx[:, None] >= tile_end[None, :]).astype(jnp.int32), axis=1), N_EXPERTS - 1)
    own = (tile_e[:, None] == experts[None, :]).astype(jnp.int32)
    left = jnp.sum(own * (counts - (t_idx[:, None] - tile_start[None, :]) * MOE_TILE), axis=1)
    tile_rows = jnp.where(t_idx < tile_end[-1], jnp.clip(left, 0, MOE_TILE), 0).astype(jnp.int32)
    n_valid = tile_end[-1:].astype(jnp.int32)
    meta = jnp.concatenate([tile_end - 1, tiles_per, n_valid]).astype(jnp.int32)
    return tile_e.astype(jnp.int32), tile_rows, n_valid, meta, pos.reshape(n_tok_tiles, TOP_K * tm), n_tiles


def _moe(lat, h2, eid, ewt, mod4, row_fn, wg, wu, wd, ln_g, ln_b, tm):
    d = lat.shape[-1]
    tile_e, tile_rows, n_valid, meta, pos, n_tiles = _route(eid)
    buf = _dispatch(meta, pos, h2.reshape(-1, LANES), n_tiles * MOE_TILE, tm)
    eout = _ffn(tile_e, tile_rows, n_valid, buf, wg, wu, wd)
    return _combine(pos, eout, ewt.transpose(0, 2, 1), lat, mod4, row_fn, ln_g, ln_b, tm)


def _swap_halves(a):
    h = a.shape[-1] // 2
    return jnp.concatenate([a[..., h:], a[..., :h]], axis=-1)


def _layer_weights(l, w_in, gm_ln_g, gm_ln_b, gm_ws, gm_bs, q_norm, w_uq, kv_norm, w_ukv, w_proj_a, w_proj_b,
                   b_merge, w_out, ln1_g, ln1_b):
    wi = w_in[l]
    kr = wi[:, OFF_KR:OFF_G]
    uq = w_uq[l].reshape(Q_LORA, N_HEADS, QK_NOPE + QK_ROPE)
    uq_r = uq[..., QK_NOPE:]
    uq_ext = jnp.concatenate([uq[..., :QK_NOPE], uq_r, _swap_halves(uq_r)], axis=-1)
    return dict(
        wu=wi[:, 0:GM_WIDTH].astype(BF16),
        wv=wi[:, GM_WIDTH:2 * GM_WIDTH].astype(BF16),
        wq=wi[:, OFF_Q:OFF_KV].astype(BF16),
        wkv=wi[:, OFF_KV:OFF_KR].astype(BF16),
        wkr=jnp.concatenate([kr, _swap_halves(kr)], axis=-1).astype(BF16),
        wg=wi[:, OFF_G:].astype(BF16),
        lng=gm_ln_g[l].reshape(1, -1), lnb=gm_ln_b[l].reshape(1, -1),
        ws=gm_ws[l].astype(BF16), bs=gm_bs[l].reshape(GM_GROUPS, CHUNK, 1),
        qn=q_norm[l].reshape(1, -1), wuq=uq_ext.reshape(Q_LORA, N_HEADS * QK_PAD).astype(BF16),
        kvn=kv_norm[l].reshape(1, -1), wukv=w_ukv[l].astype(BF16),
        bm=b_merge[l].reshape(1, -1),
        wa=w_proj_a[l].astype(BF16), wb=w_proj_b[l].astype(BF16), wo=w_out[l].astype(BF16),
        ln1g=ln1_g[l].reshape(1, -1), ln1b=ln1_b[l].reshape(1, -1),
    )


def _rope_table(seq):
    rows = seq // GRID_W
    r = jnp.repeat(jnp.arange(rows), GRID_W).astype(F32)
    c = jnp.tile(jnp.arange(GRID_W), rows).astype(F32)
    n_freq = QK_ROPE // 4
    inv = ROPE_THETA ** (-jnp.arange(n_freq, dtype=F32) / n_freq)
    ang = jnp.concatenate([r[:, None] * inv, c[:, None] * inv], axis=-1)
    cos, sin = jnp.cos(ang), jnp.sin(ang)
    return jnp.concatenate([cos, cos, -sin, sin], axis=-1)


def kernel(x, c, ctx, c_ctx, w_ada, b_ada, w_in, gm_ln_g, gm_ln_b, gm_ws, gm_bs, q_norm, w_uq, kv_norm, w_ukv,
           w_proj_a, w_proj_b, b_merge, w_out, ln1_g, ln1_b, w_router, b_router, w_e_gate, w_e_up, w_e_down,
           ln2_g, ln2_b):
    nb, seq, d = x.shape
    depth = w_ada.shape[0]
    tm = ROW_TILE
    tq = 2 * ROW_TILE if seq % (2 * ROW_TILE) == 0 else ROW_TILE

    mod_rows = -(-(nb + 1) // 8) * 8
    cc = jnp.concatenate([c, c_ctx[None, :], jnp.zeros((mod_rows - nb - 1, d), F32)], axis=0)
    mod = _ada(cc, w_ada, b_ada)
    tab = _rope_table(seq)
    wr_hi = w_router.astype(BF16)
    wr_lo = (w_router - wr_hi.astype(F32)).astype(BF16)
    lane_pad = jnp.zeros((d, LANES - N_EXPERTS), BF16)
    wr_pad = jnp.concatenate([wr_hi, lane_pad, wr_lo, lane_pad], axis=1)
    br_col = b_router.reshape(N_EXPERTS, 1)

    def lat_row(b):
        return b

    def ctx_row(b):
        return nb

    lat, cx = x, ctx
    for l in range(depth):
        last = l == depth - 1
        w = _layer_weights(l, w_in, gm_ln_g, gm_ln_b, gm_ws, gm_bs, q_norm, w_uq, kv_norm, w_ukv,
                           w_proj_a, w_proj_b, b_merge, w_out, ln1_g, ln1_b)
        wg, wu, wd = w_e_gate[l].astype(BF16), w_e_up[l].astype(BF16), w_e_down[l].astype(BF16)
        ln2 = (ln2_g[l].reshape(1, -1), ln2_b[l].reshape(1, -1))
        mod4 = mod[l].reshape(mod_rows, 6, 1, d)

        if last:
            ck, cv = _ctx_kv(cx, mod4, ctx_row, w, tm)
        else:
            cgm, cq, ck, cv, cgate = _inproj(cx, mod4, ctx_row, w, None, tm)
        gm, q, k, v, gate = _inproj(lat, mod4, lat_row, w, tab, tq)
        att = (_attention if seq % (2 * tq) == 0 else _attention_simple)(q, [(ck, cv), (k, v)], tq)
        lat1, h2, eid, ewt = _merge(lat, gm, att, gate, mod4, lat_row, w, wr_pad, br_col, tq)
        lat = _moe(lat1, h2, eid, ewt, mod4, lat_row, wg, wu, wd, *ln2, tq)
        if not last:
            catt = _attention_simple(cq, [(ck, cv)], tm)
            cx1, ch2, ceid, cewt = _merge(cx, cgm, catt, cgate, mod4, ctx_row, w, wr_pad, br_col, tm)
            cx = _moe(cx1, ch2, ceid, cewt, mod4, ctx_row, wg, wu, wd, *ln2, tm)
    return lat
```
